```python
import math
import jax, jax.numpy as jnp
from jax import lax
import numpy as np

D_MODEL = 1024
BATCH = 16
SEQ = 256
DEPTH = 2
DEC_BATCH = 8
DEC_SEQ = 4096
PAST_LEN = 256

GRID_W = 64
ROPE_THETA = 10000.0
Q_BLOCK = 128
ATTN_HEADS = 8
ATTN_KV_HEADS = 2
HEAD_DIM = 64
ATTN_Q_W = ATTN_HEADS * HEAD_DIM
ATTN_KV_W = ATTN_KV_HEADS * HEAD_DIM
S5_WIDTH = D_MODEL // 2
S5_GROUP_CH = 16
S5_GROUPS = S5_WIDTH // S5_GROUP_CH
S5_STATE = 64
AB_IN_W = ATTN_Q_W + 2 * ATTN_KV_W + S5_WIDTH
AB_MIX_W = ATTN_Q_W + S5_WIDTH
GLA_HEADS = 4
GLA_DK = D_MODEL // 2 // GLA_HEADS
GLA_DV = D_MODEL // GLA_HEADS
GLA_RANK = 16
GLA_GATE_NORM = 16.0
GLA_CHUNK = 64
GLA_IN_W = 2 * GLA_HEADS * GLA_DK + 2 * GLA_HEADS * GLA_DV + 2 * GLA_RANK
N_EXPERTS = 16
EC_CAPACITY = 2
D_FF_EXPERT = 2 * D_MODEL
N_AB_LAYERS = (DEPTH + 1) // 2
N_GLA_LAYERS = DEPTH // 2
RMS_EPS = 1e-6

kernel_name = 'hybrid_diffusion_ctx_prefix_step'


def _rms(x, g):
    xf = x.astype(jnp.float32)
    xf = xf * lax.rsqrt(jnp.mean(xf * xf, axis=-1, keepdims=True) + RMS_EPS)
    return xf.astype(x.dtype) * g


def _axial_rope(x):
    b, t, h, dh = x.shape
    rows = t // GRID_W
    row = jnp.repeat(jnp.arange(rows), GRID_W)
    col = jnp.tile(jnp.arange(GRID_W), rows)
    half = dh // 2
    nfreq = half // 2
    inv = ROPE_THETA ** (-jnp.arange(nfreq, dtype=jnp.float32) / nfreq)

    def rot(xp, pos):
        ang = pos.astype(jnp.float32)[:, None] * inv[None, :]
        cos = jnp.cos(ang)[None, :, None, :].astype(x.dtype)
        sin = jnp.sin(ang)[None, :, None, :].astype(x.dtype)
        x1, x2 = xp[..., :nfreq], xp[..., nfreq:]
        return jnp.concatenate([x1 * cos - x2 * sin, x1 * sin + x2 * cos], axis=-1)

    return jnp.concatenate([rot(x[..., :half], row), rot(x[..., half:], col)], axis=-1)


def _block_attention(q, k, v):
    b, t, h, dh = q.shape
    n_kv = k.shape[2]
    qb = q.reshape(b, t // Q_BLOCK, Q_BLOCK, n_kv, h // n_kv, dh).transpose(1, 0, 2, 3, 4, 5)

    def one_block(qi):
        s = jnp.einsum('bqkgd,bskd->bkgqs', qi, k, preferred_element_type=jnp.float32) * (dh ** -0.5)
        p = jax.nn.softmax(s, axis=-1)
        return jnp.einsum('bkgqs,bskd->bqkgd', p.astype(v.dtype), v)

    o = lax.map(one_block, qb)
    return o.transpose(1, 0, 2, 3, 4, 5).reshape(b, t, h * dh)


def _lin_combine(e1, e2):
    a1, b1 = e1
    a2, b2 = e2
    return a1 * a2, a2 * b1 + b2


def _s5_direction(u, lam_re, lam_im, log_dt, b_re, b_im, c_re, c_im, h0):
    f32 = jnp.float32
    lam = lax.complex(lam_re.astype(f32), lam_im.astype(f32))
    dt = jnp.exp(log_dt.astype(f32))[:, None]
    lam_bar = jnp.exp(lam * dt)
    b_bar = ((lam_bar - 1.0) / lam)[..., None] * lax.complex(b_re.astype(f32), b_im.astype(f32))
    c_mat = lax.complex(c_re.astype(f32), c_im.astype(f32))
    bu = jnp.einsum('btgp,gnp->btgn', u.astype(f32).astype(jnp.complex64), b_bar)
    bu = bu.at[:, 0].add(lam_bar * h0)
    a = jnp.broadcast_to(lam_bar, bu.shape)
    _, hs = lax.associative_scan(_lin_combine, (a, bu), axis=1)
    y = jnp.einsum('gpn,btgn->btgp', c_mat, hs).real
    return y, hs[:, -1]


def _attn_s5_mixer(h, w_in, w_out, q_norm, k_norm, lam_re, lam_im, log_dt, b_re, b_im,
                   c_re, c_im, d_skip, w_glu, b_glu, ctx):
    b, t, _ = h.shape
    q, k, v, u = jnp.split(h @ w_in, [ATTN_Q_W, ATTN_Q_W + ATTN_KV_W, ATTN_Q_W + 2 * ATTN_KV_W], axis=-1)
    q = _rms(q.reshape(b, t, ATTN_HEADS, HEAD_DIM), q_norm)
    k = _rms(k.reshape(b, t, ATTN_KV_HEADS, HEAD_DIM), k_norm)
    v = v.reshape(b, t, ATTN_KV_HEADS, HEAD_DIM)
    if ctx is None:
        o_attn = _block_attention(q, k, v)
        h0 = jnp.zeros((b, 2, S5_GROUPS, S5_STATE), jnp.complex64)
    else:
        ctx_k, ctx_v, ctx_state = ctx
        keys = jnp.concatenate([ctx_k.astype(k.dtype), _axial_rope(k)], axis=1)
        vals = jnp.concatenate([ctx_v.astype(v.dtype), v], axis=1)
        o_attn = _block_attention(_axial_rope(q), keys, vals)
        st = ctx_state.astype(jnp.float32)
        h0 = lax.complex(st[..., 0], st[..., 1])
    u = u.reshape(b, t, S5_GROUPS, S5_GROUP_CH)
    y_f, s_f = _s5_direction(u, lam_re[0], lam_im[0], log_dt[0], b_re[0], b_im[0], c_re[0], c_im[0], h0[:, 0])
    y_b, s_b = _s5_direction(jnp.flip(u, 1), lam_re[1], lam_im[1], log_dt[1], b_re[1], b_im[1],
                             c_re[1], c_im[1], h0[:, 1])
    y = (y_f + jnp.flip(y_b, 1)).astype(h.dtype) + d_skip * u
    y = jax.nn.gelu(y.reshape(b, t, S5_WIDTH)) @ w_glu + b_glu
    val, gate = jnp.split(y, 2, axis=-1)
    o_s5 = val * jax.nn.sigmoid(gate)
    out = jnp.concatenate([o_attn, o_s5], axis=-1) @ w_out
    s_fin = jnp.stack([s_f, s_b], axis=1)
    state = jnp.stack([s_fin.real, s_fin.imag], axis=-1)
    return out, k, v, state


def _gla_direction(q, k, v, g, s0):
    b, t, nh, _ = q.shape
    dv = v.shape[-1]
    nc = t // GLA_CHUNK

    def chunks(z):
        return z.astype(jnp.float32).reshape(b, nc, GLA_CHUNK, nh, z.shape[-1]).transpose(1, 0, 3, 2, 4)

    mask = jnp.tril(jnp.ones((GLA_CHUNK, GLA_CHUNK), jnp.float32))

    def step(s, inp):
        qi, ki, vi, gi = inp
        cum = jnp.cumsum(gi, axis=2)
        q_dec = qi * jnp.exp(cum)
        k_inv = ki * jnp.exp(-cum)
        scores = jnp.einsum('bhik,bhjk->bhij', q_dec, k_inv) * mask
        o = jnp.einsum('bhij,bhjv->bhiv', scores, vi) + jnp.einsum('bhik,bhkv->bhiv', q_dec, s)
        last = cum[:, :, -1]
        k_to_end = ki * jnp.exp(last[:, :, None] - cum)
        s_new = jnp.exp(last)[..., None] * s + jnp.einsum('bhjk,bhjv->bhkv', k_to_end, vi)
        return s_new, o

    s_fin, o = lax.scan(step, s0, (chunks(q), chunks(k), chunks(v), chunks(g)))
    return o.transpose(1, 0, 3, 2, 4).reshape(b, t, nh, dv), s_fin


def _gla_mixer(h, w_in, w_gate_up, b_gate, o_norm, w_out, ctx_state):
    b, t, _ = h.shape
    qk_w = GLA_HEADS * GLA_DK
    v_w = GLA_HEADS * GLA_DV
    q, k, v, r, lf, lb = jnp.split(h @ w_in, [qk_w, 2 * qk_w, 2 * qk_w + v_w, 2 * qk_w + 2 * v_w,
                                               2 * qk_w + 2 * v_w + GLA_RANK], axis=-1)
    q = q.reshape(b, t, GLA_HEADS, GLA_DK) * (GLA_DK ** -0.5)
    k = k.reshape(b, t, GLA_HEADS, GLA_DK)
    v = v.reshape(b, t, GLA_HEADS, GLA_DV)

    def log_gate(lr, w, bias):
        z = (lr @ w + bias).astype(jnp.float32)
        return (jax.nn.log_sigmoid(z) / GLA_GATE_NORM).reshape(b, t, GLA_HEADS, GLA_DK)

    g_f = log_gate(lf, w_gate_up[0], b_gate[0])
    g_b = log_gate(lb, w_gate_up[1], b_gate[1])
    if ctx_state is None:
        s0 = jnp.zeros((b, 2, GLA_HEADS, GLA_DK, GLA_DV), jnp.float32)
    else:
        s0 = ctx_state.astype(jnp.float32)
    o_f, s_f = _gla_direction(q, k, v, g_f, s0[:, 0])
    o_b, s_b = _gla_direction(jnp.flip(q, 1), jnp.flip(k, 1), jnp.flip(v, 1), jnp.flip(g_b, 1), s0[:, 1])
    o = _rms(o_f + jnp.flip(o_b, 1), o_norm).astype(h.dtype)
    o = o.reshape(b, t, v_w) * jax.nn.silu(r)
    return o @ w_out, jnp.stack([s_f, s_b], axis=1)


def _expert_choice_ffn(h, w_router, w_gate, w_up, w_down):
    b, t, _ = h.shape
    cap = EC_CAPACITY * t // N_EXPERTS
    aff = jax.nn.softmax(jnp.einsum('btd,de->bte', h, w_router).astype(jnp.float32), axis=-1)
    gates, idx = lax.top_k(jnp.swapaxes(aff, 1, 2), cap)
    bidx = jnp.arange(b)[:, None, None]
    xs = h[bidx, idx]
    hid = jax.nn.silu(jnp.einsum('becd,edf->becf', xs, w_gate)) * jnp.einsum('becd,edf->becf', xs, w_up)
    out = jnp.einsum('becf,efd->becd', hid, w_down) * gates[..., None].astype(h.dtype)
    return jnp.zeros_like(h).at[bidx, idx].add(out)


def _trunk(x, cond, P, caches):
    new_k, new_v, new_s5, new_gla = [], [], [], []
    for layer in range(DEPTH):
        j = layer // 2
        mod = jax.nn.silu(cond) @ P['w_mod'][layer] + P['b_mod'][layer]
        sh1, sc1, g1, sh2, sc2, g2 = jnp.split(mod[:, None, :], 6, axis=-1)
        hmix = _rms(x, P['norm_mix'][layer]) * (1 + sc1) + sh1
        if layer % 2 == 0:
            ctx = None if caches is None else (caches['k'][:, j], caches['v'][:, j], caches['s5'][:, j])
            out, k, v, s5 = _attn_s5_mixer(
                hmix, P['ab_w_in'][j], P['ab_w_out'][j], P['attn_q_norm'][j], P['attn_k_norm'][j],
                P['s5_lambda_re'][j], P['s5_lambda_im'][j], P['s5_log_dt'][j], P['s5_b_re'][j],
                P['s5_b_im'][j], P['s5_c_re'][j], P['s5_c_im'][j], P['s5_d'][j], P['s5_w_glu'][j],
                P['s5_b_glu'][j], ctx)
            if caches is None:
                new_k.append(k)
                new_v.append(v)
                new_s5.append(s5)
        else:
            ctx = None if caches is None else caches['gla'][:, j]
            out, gla = _gla_mixer(hmix, P['gla_w_in'][j], P['gla_w_gate_up'][j], P['gla_b_gate'][j],
                                  P['gla_o_norm'][j], P['gla_w_out'][j], ctx)
            if caches is None:
                new_gla.append(gla)
        x = x + g1 * out
        hffn = _rms(x, P['norm_ffn'][layer]) * (1 + sc2) + sh2
        x = x + g2 * _expert_choice_ffn(hffn, P['w_router'][layer], P['ffn_w_gate'][layer],
                                        P['ffn_w_up'][layer], P['ffn_w_down'][layer])
    return _rms(x, P['norm_out']), new_k, new_v, new_s5, new_gla


def setup_inputs(seed: int = 0) -> dict:
    key = jax.random.key(seed)
    ks = jax.random.split(key, 40)
    f32 = jnp.float32
    D, E, F = D_MODEL, N_EXPERTS, D_FF_EXPERT
    NE, NO = N_AB_LAYERS, N_GLA_LAYERS
    G, N, P = S5_GROUPS, S5_STATE, S5_GROUP_CH

    def nrm(i, shape, scale):
        return jax.random.normal(ks[i], shape, f32) * scale

    n_idx = jnp.arange(N, dtype=f32)
    return {
        'x_prompt': nrm(0, (BATCH, SEQ, D), 1.0),
        'x_sample': nrm(1, (DEC_BATCH, DEC_SEQ, D), 1.0),
        'c': nrm(2, (DEC_BATCH, D), 1.0),
        'c_ctx': nrm(3, (D,), 1.0),
        'cache_attn_k': nrm(4, (DEC_BATCH, NE, PAST_LEN, ATTN_KV_HEADS, HEAD_DIM), 1.0),
        'cache_attn_v': nrm(5, (DEC_BATCH, NE, PAST_LEN, ATTN_KV_HEADS, HEAD_DIM), 1.0),
        'state_s5': nrm(6, (DEC_BATCH, NE, 2, G, N, 2), 0.1),
        'state_gla': nrm(7, (DEC_BATCH, NO, 2, GLA_HEADS, GLA_DK, GLA_DV), 1.0),
        'norm_mix': 1.0 + nrm(8, (DEPTH, D), 0.02),
        'norm_ffn': 1.0 + nrm(9, (DEPTH, D), 0.02),
        'w_mod': nrm(10, (DEPTH, D, 6 * D), 0.5 * D ** -0.5),
        'b_mod': nrm(11, (DEPTH, 6 * D), 0.02),
        'w_router': nrm(12, (DEPTH, D, E), D ** -0.5),
        'ffn_w_gate': nrm(13, (DEPTH, E, D, F), D ** -0.5),
        'ffn_w_up': nrm(14, (DEPTH, E, D, F), D ** -0.5),
        'ffn_w_down': nrm(15, (DEPTH, E, F, D), F ** -0.5),
        'ab_w_in': nrm(16, (NE, D, AB_IN_W), D ** -0.5),
        'ab_w_out': nrm(17, (NE, AB_MIX_W, D), AB_MIX_W ** -0.5),
        'attn_q_norm': 1.0 + nrm(18, (NE, HEAD_DIM), 0.02),
        'attn_k_norm': 1.0 + nrm(19, (NE, HEAD_DIM), 0.02),
        's5_lambda_re': -0.5 + nrm(20, (NE, 2, G, N), 0.01),
        's5_lambda_im': jnp.pi * n_idx + nrm(21, (NE, 2, G, N), 0.01),
        's5_log_dt': jax.random.uniform(ks[22], (NE, 2, G), f32, math.log(1e-3), math.log(1e-1)),
        's5_b_re': nrm(23, (NE, 2, G, N, P), (2 * P) ** -0.5),
        's5_b_im': nrm(24, (NE, 2, G, N, P), (2 * P) ** -0.5),
        's5_c_re': nrm(25, (NE, 2, G, P, N), 0.5 ** 0.5),
        's5_c_im': nrm(26, (NE, 2, G, P, N), 0.5 ** 0.5),
        's5_d': nrm(27, (NE, G, P), 1.0),
        's5_w_glu': nrm(28, (NE, S5_WIDTH, 2 * S5_WIDTH), S5_WIDTH ** -0.5),
        's5_b_glu': nrm(29, (NE, 2 * S5_WIDTH), 0.02),
        'gla_w_in': nrm(30, (NO, D, GLA_IN_W), D ** -0.5),
        'gla_w_gate_up': nrm(31, (NO, 2, GLA_RANK, GLA_HEADS * GLA_DK), GLA_RANK ** -0.5),
        'gla_b_gate': 1.0 + nrm(32, (NO, 2, GLA_HEADS * GLA_DK), 0.1),
        'gla_o_norm': 1.0 + nrm(33, (NO, GLA_DV), 0.02),
        'gla_w_out': nrm(34, (NO, GLA_HEADS * GLA_DV, D), (GLA_HEADS * GLA_DV) ** -0.5),
        'norm_out': 1.0 + nrm(35, (D,), 0.02),
    }


def reference(x_prompt, x_sample, c, c_ctx, cache_attn_k, cache_attn_v, state_s5, state_gla,
              norm_mix, norm_ffn, w_mod, b_mod, w_router, ffn_w_gate, ffn_w_up, ffn_w_down,
              ab_w_in, ab_w_out, attn_q_norm, attn_k_norm, s5_lambda_re, s5_lambda_im, s5_log_dt,
              s5_b_re, s5_b_im, s5_c_re, s5_c_im, s5_d, s5_w_glu, s5_b_glu,
              gla_w_in, gla_w_gate_up, gla_b_gate, gla_o_norm, gla_w_out, norm_out):
    P = {
        'norm_mix': norm_mix, 'norm_ffn': norm_ffn, 'w_mod': w_mod, 'b_mod': b_mod,
        'w_router': w_router, 'ffn_w_gate': ffn_w_gate, 'ffn_w_up': ffn_w_up, 'ffn_w_down': ffn_w_down,
        'ab_w_in': ab_w_in, 'ab_w_out': ab_w_out, 'attn_q_norm': attn_q_norm, 'attn_k_norm': attn_k_norm,
        's5_lambda_re': s5_lambda_re, 's5_lambda_im': s5_lambda_im, 's5_log_dt': s5_log_dt,
        's5_b_re': s5_b_re, 's5_b_im': s5_b_im, 's5_c_re': s5_c_re, 's5_c_im': s5_c_im,
        's5_d': s5_d, 's5_w_glu': s5_w_glu, 's5_b_glu': s5_b_glu,
        'gla_w_in': gla_w_in, 'gla_w_gate_up': gla_w_gate_up, 'gla_b_gate': gla_b_gate,
        'gla_o_norm': gla_o_norm, 'gla_w_out': gla_w_out, 'norm_out': norm_out,
    }
    y_prompt, ks, vs, s5s, glas = _trunk(x_prompt, c_ctx[None, :], P, None)
    new_cache_attn_k = jnp.stack(ks, axis=1)
    new_cache_attn_v = jnp.stack(vs, axis=1)
    new_state_s5 = jnp.stack(s5s, axis=1)
    new_state_gla = jnp.stack(glas, axis=1)
    caches = {'k': cache_attn_k, 'v': cache_attn_v, 's5': state_s5, 'gla': state_gla}
    y_sample, _, _, _, _ = _trunk(x_sample, c, P, caches)
    return (y_prompt, y_sample, new_cache_attn_k, new_cache_attn_v, new_state_s5, new_state_gla)
```

```python
import functools
import math

import jax
import jax.numpy as jnp
from jax import lax
from jax.experimental import pallas as pl
from jax.experimental.pallas import tpu as pltpu

F32 = jnp.float32
BF16 = jnp.bfloat16

RMS_EPS = 1e-6
GRID_W = 64
ROPE_THETA = 10000.0
ATTN_HEADS = 8
ATTN_KV_HEADS = 2
HEAD_DIM = 64
S5_GROUP_CH = 16
S5_CHUNK = 16
GLA_HEADS = 4
GLA_RANK = 16
GLA_GATE_NORM = 16.0
GLA_CHUNK = 64
EC_CAPACITY = 2
LANES = 128
TOKEN_TILE = 512
VMEM_LIMIT = 56 * 1024 * 1024


def _cparams(*sem):
    return pltpu.CompilerParams(dimension_semantics=sem, vmem_limit_bytes=VMEM_LIMIT)


def _dot(a, b):
    return jnp.dot(a, b, preferred_element_type=F32)


def _dot_nt(a, b):
    return lax.dot_general(a, b, (((1,), (1,)), ((), ())), preferred_element_type=F32)


def _split(a):
    hi = a.astype(BF16)
    lo = (a - hi.astype(F32)).astype(BF16)
    return hi, lo


def _dot3(a, b):
    ah, al = _split(a)
    bh, bl = _split(b)
    return _dot(ah, bh) + _dot(al, bh) + _dot(ah, bl)


def _dot2(a, b_exact):
    ah, al = _split(a)
    return _dot(ah, b_exact) + _dot(al, b_exact)


def _sigmoid(x):
    return 1.0 / (1.0 + jnp.exp(-x))


def _silu(x):
    return x * _sigmoid(x)


def _gelu_tanh(x):
    c = math.sqrt(2.0 / math.pi)
    return 0.5 * x * (1.0 + jnp.tanh(c * (x + 0.044715 * (x * x * x))))


def _log_sigmoid(z):
    return jnp.minimum(z, 0.0) - jnp.log1p(jnp.exp(-jnp.abs(z)))


def _rms_mod(x, gain, scale, shift):
    ms = jnp.mean(x * x, axis=-1, keepdims=True)
    return (x * lax.rsqrt(ms + RMS_EPS)) * gain * (1.0 + scale) + shift


def _router_probs(xn, mod, gain, wr, ne):
    h = _rms_mod(xn, gain, mod[4:5], mod[3:4])
    logits = _dot3(h, wr)
    lane = lax.broadcasted_iota(jnp.int32, logits.shape, 1)
    logits = jnp.where(lane < ne, logits, -jnp.inf)
    m = jnp.max(logits, axis=-1, keepdims=True)
    p = jnp.exp(logits - m)
    return p / jnp.sum(p, axis=-1, keepdims=True)


def _mod_kernel(c_ref, w_ref, b_ref, o_ref):
    c = c_ref[...]
    o_ref[0] = _dot3(_silu(c), w_ref[0]) + b_ref[0]


def _modulation(cond, w_mod, b_mod):
    depth, d, n = w_mod.shape
    r = cond.shape[0]
    tn = n // 4
    return pl.pallas_call(
        _mod_kernel,
        name="modulation",
        grid=(depth, n // tn),
        in_specs=[
            pl.BlockSpec((r, d), lambda l, j: (0, 0)),
            pl.BlockSpec((1, d, tn), lambda l, j: (l, 0, j)),
            pl.BlockSpec((1, 1, tn), lambda l, j: (l, 0, j)),
        ],
        out_specs=pl.BlockSpec((1, r, tn), lambda l, j: (l, 0, j)),
        out_shape=jax.ShapeDtypeStruct((depth, r, n), F32),
        compiler_params=_cparams("parallel", "parallel"),
    )(cond, w_mod, b_mod.reshape(depth, 1, n))


def _inproj_ab_kernel(x_ref, mod_ref, g_ref, w_ref, o_ref, *, nsplit):
    m = mod_ref[0]
    h = _rms_mod(x_ref[...], g_ref[...], m[1:2], m[0:1]).astype(BF16)
    n = o_ref.shape[1]
    w = n // nsplit
    for s in range(nsplit):
        o_ref[:, s * w:(s + 1) * w] = _dot(h, w_ref[:, s * w:(s + 1) * w])


def _inproj_ab(x, mod, gain, w, *, tps, tm):
    ntok, d = x.shape
    n = w.shape[1]
    return pl.pallas_call(
        functools.partial(_inproj_ab_kernel, nsplit=2),
        name="inproj_ab",
        grid=(ntok // tm,),
        in_specs=[
            pl.BlockSpec((tm, d), lambda i: (i, 0)),
            pl.BlockSpec((1, 8, d), lambda i: (i // tps, 0, 0)),
            pl.BlockSpec((1, d), lambda i: (0, 0)),
            pl.BlockSpec((d, n), lambda i: (0, 0)),
        ],
        out_specs=pl.BlockSpec((tm, n), lambda i: (i, 0)),
        out_shape=jax.ShapeDtypeStruct((ntok, n), F32),
        compiler_params=_cparams("parallel"),
    )(x, mod, gain, w)


def _inproj_gla_kernel(x_ref, mod_ref, g_ref, w_ref, wl_ref, wg_ref, bg_ref, o_ref, gate_ref, *, nsplit):
    m = mod_ref[0]
    h = _rms_mod(x_ref[...], g_ref[...], m[1:2], m[0:1]).astype(BF16)
    n = o_ref.shape[1]
    w = n // nsplit
    for s in range(nsplit):
        o_ref[:, s * w:(s + 1) * w] = _dot(h, w_ref[:, s * w:(s + 1) * w])
    low = _dot(h, wl_ref[...])
    z = _dot3(low, wg_ref[...]) + bg_ref[...]
    gate_ref[...] = _log_sigmoid(z) / GLA_GATE_NORM


def _inproj_gla(x, mod, gain, w, wl, wg, bg, *, tps, tm):
    ntok, d = x.shape
    n = w.shape[1]
    ng = wg.shape[1]
    return pl.pallas_call(
        functools.partial(_inproj_gla_kernel, nsplit=n // 512),
        name="inproj_gla",
        grid=(ntok // tm,),
        in_specs=[
            pl.BlockSpec((tm, d), lambda i: (i, 0)),
            pl.BlockSpec((1, 8, d), lambda i: (i // tps, 0, 0)),
            pl.BlockSpec((1, d), lambda i: (0, 0)),
            pl.BlockSpec((d, n), lambda i: (0, 0)),
            pl.BlockSpec((d, LANES), lambda i: (0, 0)),
            pl.BlockSpec((LANES, ng), lambda i: (0, 0)),
            pl.BlockSpec((1, ng), lambda i: (0, 0)),
        ],
        out_specs=[pl.BlockSpec((tm, n), lambda i: (i, 0)),
                   pl.BlockSpec((tm, ng), lambda i: (i, 0))],
        out_shape=[jax.ShapeDtypeStruct((ntok, n), F32),
                   jax.ShapeDtypeStruct((ntok, ng), F32)],
        compiler_params=_cparams("parallel"),
    )(x, mod, gain, w, wl, wg, bg)


def _qkprep_kernel(q_ref, k_ref, v_ref, cos_ref, sin_ref, qg_ref, kg_ref, oq_ref, ok_ref,
                   qo_ref, ko_ref, kn_ref, vo_ref):
    cos = cos_ref[0]
    sin = sin_ref[0]

    def prep(x, gain, ones):
        width = x.shape[1]
        ssq = _dot2(x * x, ones)
        xn = x * lax.rsqrt(ssq * (1.0 / HEAD_DIM) + RMS_EPS) * gain
        lane = lax.broadcasted_iota(jnp.int32, x.shape, 1)
        first = (lane & 16) == 0
        partner = jnp.where(first, pltpu.roll(xn, width - 16, 1), pltpu.roll(xn, 16, 1))
        rep = width // LANES
        cosw = jnp.concatenate([cos] * rep, axis=1) if rep > 1 else cos
        sinw = jnp.concatenate([sin] * rep, axis=1) if rep > 1 else sin
        return xn, xn * cosw + partner * sinw

    _, qr = prep(q_ref[...], qg_ref[...], oq_ref[...])
    qo_ref[...] = (qr * (HEAD_DIM ** -0.5)).astype(BF16)
    kn, kr = prep(k_ref[...], kg_ref[...], ok_ref[...])
    kn_ref[...] = kn
    ko_ref[...] = kr.astype(BF16)
    vo_ref[...] = v_ref[...].astype(BF16)


def _qkprep(proj, cos_tab, sin_tab, q_gain, k_gain, *, ts, tm, n_rope_seq):
    ntok = proj.shape[0]
    qw = ATTN_HEADS * HEAD_DIM
    kw = ATTN_KV_HEADS * HEAD_DIM
    tps = ts // tm

    def head_ones(width):
        a = jnp.arange(width) // HEAD_DIM
        return (a[:, None] == a[None, :]).astype(BF16)

    def tab_map(i):
        return (jnp.minimum(i // tps, n_rope_seq) // n_rope_seq, i % tps, 0)

    return pl.pallas_call(
        _qkprep_kernel,
        name="qkprep",
        grid=(ntok // tm,),
        in_specs=[
            pl.BlockSpec((tm, qw), lambda i: (i, 0)),
            pl.BlockSpec((tm, kw), lambda i: (i, qw // kw)),
            pl.BlockSpec((tm, kw), lambda i: (i, qw // kw + 1)),
            pl.BlockSpec((1, tm, LANES), tab_map),
            pl.BlockSpec((1, tm, LANES), tab_map),
            pl.BlockSpec((1, qw), lambda i: (0, 0)),
            pl.BlockSpec((1, kw), lambda i: (0, 0)),
            pl.BlockSpec((qw, qw), lambda i: (0, 0)),
            pl.BlockSpec((kw, kw), lambda i: (0, 0)),
        ],
        out_specs=[
            pl.BlockSpec((tm, qw), lambda i: (i, 0)),
            pl.BlockSpec((tm, kw), lambda i: (i, 0)),
            pl.BlockSpec((tm, kw), lambda i: (i, 0)),
            pl.BlockSpec((tm, kw), lambda i: (i, 0)),
        ],
        out_shape=[
            jax.ShapeDtypeStruct((ntok, qw), BF16),
            jax.ShapeDtypeStruct((ntok, kw), BF16),
            jax.ShapeDtypeStruct((ntok, kw), F32),
            jax.ShapeDtypeStruct((ntok, kw), BF16),
        ],
        compiler_params=_cparams("parallel"),
    )(proj, proj, proj, cos_tab, sin_tab,
      jnp.tile(q_gain, ATTN_HEADS).reshape(1, qw), jnp.tile(k_gain, ATTN_KV_HEADS).reshape(1, kw),
      head_ones(qw), head_ones(kw))


def _rope_tables(t):
    nfreq = HEAD_DIM // 4
    inv = ROPE_THETA ** (-jnp.arange(nfreq, dtype=F32) / nfreq)
    pos = jnp.arange(t)
    row = (pos // GRID_W).astype(F32)[:, None] * inv[None, :]
    col = (pos % GRID_W).astype(F32)[:, None] * inv[None, :]
    cos = jnp.concatenate([jnp.cos(row), jnp.cos(row), jnp.cos(col), jnp.cos(col)], axis=1)
    sin = jnp.concatenate([-jnp.sin(row), jnp.sin(row), -jnp.sin(col), jnp.sin(col)], axis=1)
    rep = LANES // HEAD_DIM
    cos = jnp.tile(cos, (1, rep))
    sin = jnp.tile(sin, (1, rep))
    return (jnp.stack([cos, jnp.ones_like(cos)]), jnp.stack([sin, jnp.zeros_like(sin)]))


def _attn_kernel(q_ref, kt_ref, v_ref, *rest):
    o_ref = rest[-1]
    q = q_ref[...]
    v = v_ref[0]
    group = ATTN_HEADS // ATTN_KV_HEADS
    outs = []
    for h in range(ATTN_HEADS):
        kv = h // group
        qh = q[:, h * HEAD_DIM:(h + 1) * HEAD_DIM]
        kt = kt_ref[0, kv * HEAD_DIM:(kv + 1) * HEAD_DIM, :]
        s = _dot(qh, kt)
        p = jnp.exp(s - jnp.max(s, axis=-1, keepdims=True))
        l = jnp.sum(p, axis=-1, keepdims=True)
        o = _dot(p.astype(BF16), v)
        outs.append(o[:, kv * HEAD_DIM:(kv + 1) * HEAD_DIM] / l)
    o_ref[...] = jnp.concatenate(outs, axis=1).astype(BF16)


def _attention(q, keys_t, vals, prev, *, row0, t, tq):
    b, kw, s = keys_t.shape
    ntok, qw = q.shape
    nq = t // tq
    rb0 = row0 // tq
    extra = [] if prev is None else [prev]
    return pl.pallas_call(
        _attn_kernel,
        name="attention",
        grid=(b, nq),
        in_specs=[
            pl.BlockSpec((tq, qw), lambda bi, i: (rb0 + bi * nq + i, 0)),
            pl.BlockSpec((1, kw, s), lambda bi, i: (bi, 0, 0)),
            pl.BlockSpec((1, s, kw), lambda bi, i: (bi, 0, 0)),
        ] + [pl.BlockSpec(memory_space=pl.ANY)] * len(extra),
        out_specs=pl.BlockSpec((tq, qw), lambda bi, i: (rb0 + bi * nq + i, 0)),
        out_shape=jax.ShapeDtypeStruct((ntok, qw), BF16),
        input_output_aliases={3: 0} if extra else {},
        compiler_params=_cparams("parallel", "parallel"),
    )(q, keys_t, vals, *extra)


def _s5_kernel(x_ref, e_ref, m_ref, f_ref, lam_ref, h0_ref, y_ref, hfin_ref, s_ref, hin_ref, *, nb, nc):
    x = x_ref[0]
    s_ref[...] = _dot3(x, e_ref[0])
    lam = lam_ref[0]
    lfr = jnp.broadcast_to(lam[0:1], (nb, LANES))
    lfi = jnp.broadcast_to(lam[1:2], (nb, LANES))
    lbr = jnp.broadcast_to(lam[2:3], (nb, LANES))
    lbi = jnp.broadcast_to(lam[3:4], (nb, LANES))

    def step(c, carry):
        hfr, hfi, hbr, hbi = carry
        rf = pl.multiple_of(c * nb, nb)
        rb = pl.multiple_of((nc - 1 - c) * nb, nb)
        hin_ref[pl.ds(rf, nb), 0:LANES] = hfr
        hin_ref[pl.ds(rf, nb), LANES:2 * LANES] = hfi
        hin_ref[pl.ds(rb, nb), 2 * LANES:3 * LANES] = hbr
        hin_ref[pl.ds(rb, nb), 3 * LANES:4 * LANES] = hbi
        sfr = s_ref[pl.ds(rf, nb), 0:LANES]
        sfi = s_ref[pl.ds(rf, nb), LANES:2 * LANES]
        sbr = s_ref[pl.ds(rb, nb), 2 * LANES:3 * LANES]
        sbi = s_ref[pl.ds(rb, nb), 3 * LANES:4 * LANES]
        return (lfr * hfr - lfi * hfi + sfr, lfr * hfi + lfi * hfr + sfi,
                lbr * hbr - lbi * hbi + sbr, lbr * hbi + lbi * hbr + sbi)

    h0 = h0_ref[0]
    hfr, hfi, hbr, hbi = lax.fori_loop(0, nc, step, (h0[0], h0[1], h0[2], h0[3]))
    hfin_ref[0, 0] = hfr
    hfin_ref[0, 1] = hfi
    hfin_ref[0, 2] = hbr
    hfin_ref[0, 3] = hbi
    y_ref[0] = _dot3(x, m_ref[0]) + _dot3(hin_ref[...], f_ref[0])


def _s5(xg, e_mat, m_mat, f_mat, lam_l, h0, *, nb, nc):
    g, r, w = xg.shape
    sw = 4 * LANES
    return pl.pallas_call(
        functools.partial(_s5_kernel, nb=nb, nc=nc),
        name="s5",
        grid=(g,),
        in_specs=[
            pl.BlockSpec((1, r, w), lambda i: (i, 0, 0)),
            pl.BlockSpec((1, w, sw), lambda i: (i, 0, 0)),
            pl.BlockSpec((1, w, w), lambda i: (i, 0, 0)),
            pl.BlockSpec((1, sw, w), lambda i: (i, 0, 0)),
            pl.BlockSpec((1, 8, LANES), lambda i: (i, 0, 0)),
            pl.BlockSpec((1, 4, nb, LANES), lambda i: (i, 0, 0, 0)),
        ],
        out_specs=[pl.BlockSpec((1, r, w), lambda i: (i, 0, 0)),
                   pl.BlockSpec((1, 4, nb, LANES), lambda i: (i, 0, 0, 0))],
        out_shape=[jax.ShapeDtypeStruct((g, r, w), F32),
                   jax.ShapeDtypeStruct((g, 4, nb, LANES), F32)],
        scratch_shapes=[pltpu.VMEM((r, sw), F32), pltpu.VMEM((r, sw), F32)],
        compiler_params=_cparams("parallel"),
    )(xg, e_mat, m_mat, f_mat, lam_l, h0)


def _s5_matrices(lam_re, lam_im, log_dt, b_re, b_im, c_re, c_im, d_skip):
    hi = lax.Precision.HIGHEST
    L = S5_CHUNK
    _, g, n = lam_re.shape
    p = b_re.shape[-1]
    dt = jnp.exp(log_dt)[..., None]
    mag = jnp.exp(lam_re * dt)
    lbr, lbi = mag * jnp.cos(lam_im * dt), mag * jnp.sin(lam_im * dt)
    den = lam_re * lam_re + lam_im * lam_im
    cr = ((lbr - 1.0) * lam_re + lbi * lam_im) / den
    ci = (lbi * lam_re - (lbr - 1.0) * lam_im) / den
    bbr = cr[..., None] * b_re - ci[..., None] * b_im
    bbi = cr[..., None] * b_im + ci[..., None] * b_re
    pr, pi = [jnp.ones_like(lbr)], [jnp.zeros_like(lbr)]
    for _ in range(L):
        pr, pi = pr + [pr[-1] * lbr - pi[-1] * lbi], pi + [pr[-1] * lbi + pi[-1] * lbr]
    pwr, pwi = jnp.stack(pr, axis=2), jnp.stack(pi, axis=2)
    cpr = c_re[:, :, None] * pwr[:, :, :, None, :] - c_im[:, :, None] * pwi[:, :, :, None, :]
    cpi = c_re[:, :, None] * pwi[:, :, :, None, :] + c_im[:, :, None] * pwr[:, :, :, None, :]
    kern = (jnp.einsum('dgkpn,dgnq->dgkpq', cpr[:, :, :L], bbr, precision=hi)
            - jnp.einsum('dgkpn,dgnq->dgkpq', cpi[:, :, :L], bbi, precision=hi))
    step = jnp.arange(L)
    lag_f = step[None, :] - step[:, None]
    lag_b = -lag_f

    def toeplitz(kd, lag):
        t = jnp.take(kd, jnp.clip(lag, 0, L - 1), axis=1)
        t = jnp.where((lag >= 0)[None, :, :, None, None], t, 0.0)
        return t.transpose(0, 1, 4, 2, 3).reshape(g, L * p, L * p)

    eye_l = jnp.eye(L, dtype=F32)
    dmat = jnp.einsum('ji,gq,qp->gjqip', eye_l, d_skip, jnp.eye(p, dtype=F32)).reshape(g, L * p, L * p)
    m_mat = toeplitz(kern[0], lag_f) + toeplitz(kern[1], lag_b) + dmat

    def e_block(pw_r, pw_i, b_r, b_i):
        er = pw_r[:, :, None, :] * b_r.transpose(0, 2, 1)[:, None] - pw_i[:, :, None, :] * b_i.transpose(0, 2, 1)[:, None]
        ei = pw_r[:, :, None, :] * b_i.transpose(0, 2, 1)[:, None] + pw_i[:, :, None, :] * b_r.transpose(0, 2, 1)[:, None]
        return er.reshape(g, L * p, n), ei.reshape(g, L * p, n)

    def pad_cols(a):
        return jnp.pad(a, ((0, 0), (0, 0), (0, LANES - n)))

    efr, efi = e_block(pwr[0][:, L - 1 - step], pwi[0][:, L - 1 - step], bbr[0], bbi[0])
    ebr, ebi = e_block(pwr[1][:, step], pwi[1][:, step], bbr[1], bbi[1])
    e_mat = jnp.concatenate([pad_cols(efr), pad_cols(efi), pad_cols(ebr), pad_cols(ebi)], axis=2)

    def f_block(c_r, c_i):
        fr = c_r.transpose(0, 3, 1, 2).reshape(g, n, L * p)
        fi = -c_i.transpose(0, 3, 1, 2).reshape(g, n, L * p)
        return fr, fi

    def pad_rows(a):
        return jnp.pad(a, ((0, 0), (0, LANES - n), (0, 0)))

    ffr, ffi = f_block(cpr[0][:, step + 1], cpi[0][:, step + 1])
    fbr, fbi = f_block(cpr[1][:, L - step], cpi[1][:, L - step])
    f_mat = jnp.concatenate([pad_rows(ffr), pad_rows(ffi), pad_rows(fbr), pad_rows(fbi)], axis=1)

    lam_l = jnp.stack([pwr[0, :, L], pwi[0, :, L], pwr[1, :, L], pwi[1, :, L]], axis=1)
    lam_l = jnp.pad(lam_l, ((0, 0), (0, 4), (0, LANES - n)))
    return e_mat, m_mat, f_mat, lam_l


def _s5_to_chunks(u, nb, t, g):
    p = u.shape[1] // g
    nc = t // S5_CHUNK
    x = u.reshape(nb, nc, S5_CHUNK, g, p).transpose(3, 1, 0, 2, 4)
    return x.reshape(g, nc * nb, S5_CHUNK * p)


def _s5_from_chunks(y, nb, t):
    g, _, w = y.shape
    p = w // S5_CHUNK
    nc = t // S5_CHUNK
    return y.reshape(g, nc, nb, S5_CHUNK, p).transpose(2, 1, 3, 0, 4).reshape(nb * t, g * p)


def _ab_post_kernel(y_ref, oa_ref, x_ref, mod_ref, wglu_ref, bglu_ref, woa_ref, wos_ref, gffn_ref, wr_ref,
                    xo_ref, aff_ref, *, ne):
    m = mod_ref[0]
    half = wos_ref.shape[0]
    glu = _dot(_gelu_tanh(y_ref[...]).astype(BF16), wglu_ref[...]) + bglu_ref[...]
    o_s5 = glu[:, :half] * _sigmoid(glu[:, half:])
    out = _dot(oa_ref[...], woa_ref[...]) + _dot(o_s5.astype(BF16), wos_ref[...])
    xn = x_ref[...] + m[2:3] * out
    xo_ref[...] = xn
    aff_ref[...] = _router_probs(xn, m, gffn_ref[...], wr_ref[...], ne)


def _ab_post(y, o_attn, x, mod, w_glu, b_glu, w_out_a, w_out_s, g_ffn, w_router, *, tps, tm, ne):
    ntok, d = x.shape
    hw = y.shape[1]
    full = lambda shape: pl.BlockSpec(shape, lambda i: (0,) * len(shape))
    return pl.pallas_call(
        functools.partial(_ab_post_kernel, ne=ne),
        name="ab_post",
        grid=(ntok // tm,),
        in_specs=[
            pl.BlockSpec((tm, hw), lambda i: (i, 0)),
            pl.BlockSpec((tm, o_attn.shape[1]), lambda i: (i, 0)),
            pl.BlockSpec((tm, d), lambda i: (i, 0)),
            pl.BlockSpec((1, 8, d), lambda i: (i // tps, 0, 0)),
            full(w_glu.shape), full(b_glu.shape), full(w_out_a.shape), full(w_out_s.shape),
            full(g_ffn.shape), full(w_router.shape),
        ],
        out_specs=[pl.BlockSpec((tm, d), lambda i: (i, 0)),
                   pl.BlockSpec((tm, LANES), lambda i: (i, 0))],
        out_shape=[jax.ShapeDtypeStruct((ntok, d), F32),
                   jax.ShapeDtypeStruct((ntok, LANES), F32)],
        compiler_params=_cparams("parallel"),
    )(y, o_attn, x, mod, w_glu, b_glu, w_out_a, w_out_s, g_ffn, w_router)


def _gla_post_kernel(of_ref, ob_ref, r_ref, x_ref, mod_ref, gn_ref, wo_ref, gffn_ref, wr_ref, xo_ref, aff_ref,
                     *, ne):
    m = mod_ref[0]
    o = of_ref[0] + ob_ref[0]
    dv = gn_ref.shape[1]
    parts = []
    for h in range(GLA_HEADS):
        oh = o[:, h * dv:(h + 1) * dv]
        ms = jnp.mean(oh * oh, axis=-1, keepdims=True)
        parts.append(oh * lax.rsqrt(ms + RMS_EPS) * gn_ref[...])
    on = jnp.concatenate(parts, axis=1) * _silu(r_ref[...])
    out = _dot(on.astype(BF16), wo_ref[...])
    xn = x_ref[...] + m[2:3] * out
    xo_ref[...] = xn
    aff_ref[...] = _router_probs(xn, m, gffn_ref[...], wr_ref[...], ne)


def _gla_post(o_dirs, proj, x, mod, g_norm, w_out, g_ffn, w_router, *, tps, tm, r_col, ne):
    ntok, d = x.shape
    vw = o_dirs.shape[2]
    full = lambda shape: pl.BlockSpec(shape, lambda i: (0,) * len(shape))
    return pl.pallas_call(
        functools.partial(_gla_post_kernel, ne=ne),
        name="gla_post",
        grid=(ntok // tm,),
        in_specs=[
            pl.BlockSpec((1, tm, vw), lambda i: (0, i, 0)),
            pl.BlockSpec((1, tm, vw), lambda i: (1, i, 0)),
            pl.BlockSpec((tm, vw), lambda i: (i, r_col // vw)),
            pl.BlockSpec((tm, d), lambda i: (i, 0)),
            pl.BlockSpec((1, 8, d), lambda i: (i // tps, 0, 0)),
            full(g_norm.shape), full(w_out.shape), full(g_ffn.shape), full(w_router.shape),
        ],
        out_specs=[pl.BlockSpec((tm, d), lambda i: (i, 0)),
                   pl.BlockSpec((tm, LANES), lambda i: (i, 0))],
        out_shape=[jax.ShapeDtypeStruct((ntok, d), F32),
                   jax.ShapeDtypeStruct((ntok, LANES), F32)],
        compiler_params=_cparams("parallel"),
    )(o_dirs, o_dirs, proj, x, mod, g_norm, w_out, g_ffn, w_router)


def _gla_kernel(q_ref, k_ref, v_ref, g_ref, s0_ref, *rest, nchunk, dk):
    o_ref, sfin_ref, st_ref = rest[-3:]
    d = pl.program_id(2)
    j = pl.program_id(3)
    c = GLA_CHUNK

    @pl.when(j == 0)
    def _():
        st_ref[...] = s0_ref[0, 0, 0]

    row = lax.broadcasted_iota(jnp.int32, (c, c), 0)
    col = lax.broadcasted_iota(jnp.int32, (c, c), 1)
    tri = jnp.where(d == 0, row - col, col - row) >= 0
    tri_f = tri.astype(F32)
    tri_b = tri.astype(BF16)

    def step(i, carry):
        cl = jnp.where(d == 0, i, nchunk - 1 - i)
        r0 = pl.multiple_of(cl * c, c)
        qi = q_ref[pl.ds(r0, c), :] * (dk ** -0.5)
        ki = k_ref[pl.ds(r0, c), :]
        vi = v_ref[pl.ds(r0, c), :]
        gi = g_ref[pl.ds(r0, c), :]
        cum = _dot_tri(tri_b, gi)
        tot = jnp.sum(gi, axis=0, keepdims=True)
        q_dec = (qi * jnp.exp(cum)).astype(BF16)
        k_inv = (ki * jnp.exp(-cum)).astype(BF16)
        k_end = (ki * jnp.exp(tot - cum)).astype(BF16)
        vb = vi.astype(BF16)
        scores = _dot_nt(q_dec, k_inv) * tri_f
        st = st_ref[...]
        o = _dot(scores.astype(BF16), vb) + _dot_nt(q_dec, st.astype(BF16))
        o_ref[0, pl.ds(r0, c), :] = o
        st_ref[...] = st * jnp.exp(tot) + _dot(vi.T.astype(BF16), k_end)
        return carry

    lax.fori_loop(0, nchunk, step, 0)

    @pl.when(j == pl.num_programs(3) - 1)
    def _():
        sfin_ref[0, 0, 0] = st_ref[...]


def _dot_tri(tri_b, g):
    gh, gl = _split(g)
    return _dot(tri_b, gh) + _dot(tri_b, gl)


def _gla(proj, gates, s0t, prev, *, row0, nb, t, tb, dk, dv):
    h = GLA_HEADS
    ntok = proj.shape[0]
    nblk = t // tb
    rb0 = row0 // tb

    def rows(b, blk, d):
        return rb0 + b * nblk + blk + d * (nblk - 1 - 2 * blk)

    extra = [] if prev is None else [prev]
    return pl.pallas_call(
        functools.partial(_gla_kernel, nchunk=tb // GLA_CHUNK, dk=dk),
        name="gla",
        input_output_aliases={5: 0} if extra else {},
        grid=(nb, h, 2, nblk),
        in_specs=[
            pl.BlockSpec((tb, dk), lambda b, hh, d, j: (rows(b, j, d), hh)),
            pl.BlockSpec((tb, dk), lambda b, hh, d, j: (rows(b, j, d), h + hh)),
            pl.BlockSpec((tb, dv), lambda b, hh, d, j: (rows(b, j, d), (2 * h * dk) // dv + hh)),
            pl.BlockSpec((tb, dk), lambda b, hh, d, j: (rows(b, j, d), d * h + hh)),
            pl.BlockSpec((1, 1, 1, dv, dk), lambda b, hh, d, j: (b, d, hh, 0, 0)),
        ] + [pl.BlockSpec(memory_space=pl.ANY)] * len(extra),
        out_specs=[
            pl.BlockSpec((1, tb, dv), lambda b, hh, d, j: (d, rows(b, j, d), hh)),
            pl.BlockSpec((1, 1, 1, dv, dk), lambda b, hh, d, j: (b, d, hh, 0, 0)),
        ],
        out_shape=[jax.ShapeDtypeStruct((2, ntok, h * dv), F32),
                   jax.ShapeDtypeStruct((nb, 2, h, dv, dk), F32)],
        scratch_shapes=[pltpu.VMEM((dv, dk), F32)],
        compiler_params=_cparams("parallel", "parallel", "parallel", "arbitrary"),
    )(proj, proj, proj, gates, s0t, *extra)


def _moe_kernel(idx_ref, x_ref, gate_ref, mod_ref, gffn_ref, wg_ref, wu_ref, wd_ref, o_ref, xs_ref, acc_ref,
                *, cap):
    e = pl.program_id(1)
    f = pl.program_id(2)
    m = mod_ref[0]

    @pl.when((e == 0) & (f == 0))
    def _():
        o_ref[...] = x_ref[...]

    @pl.when(f == 0)
    def _():
        def gather(r, carry):
            acc_ref[pl.ds(r, 1), :] = x_ref[0, pl.ds(idx_ref[0, 0, 0, r], 1), :]
            return carry
        lax.fori_loop(0, cap, gather, 0, unroll=8)
        xs_ref[...] = _rms_mod(acc_ref[...], gffn_ref[...], m[4:5], m[3:4]).astype(BF16)

    xs = xs_ref[...]
    hid = _silu(_dot(xs, wg_ref[0])) * _dot(xs, wu_ref[0])
    part = _dot(hid.astype(BF16), wd_ref[0])

    @pl.when(f == 0)
    def _():
        acc_ref[...] = part

    @pl.when(f > 0)
    def _():
        acc_ref[...] += part

    @pl.when(f == pl.num_programs(2) - 1)
    def _():
        acc_ref[...] = acc_ref[...] * gate_ref[0, 0] * m[5:6]

        def scatter(r, carry):
            i = idx_ref[0, 0, 0, r]
            o_ref[0, pl.ds(i, 1), :] += acc_ref[pl.ds(r, 1), :]
            return carry
        lax.fori_loop(0, cap, scatter, 0, unroll=8)


def _moe(x3, idx, gates, mod, g_ffn, w_gate, w_up, w_down, *, tf):
    ns, ts, d = x3.shape
    ne, _, ff = w_gate.shape
    cap = idx.shape[2]
    return pl.pallas_call(
        functools.partial(_moe_kernel, cap=cap),
        name="moe",
        grid=(ns, ne, ff // tf),
        in_specs=[
            pl.BlockSpec((1, 1, 1, cap), lambda s, e, f: (s, e, 0, 0), memory_space=pltpu.SMEM),
            pl.BlockSpec((1, ts, d), lambda s, e, f: (s, 0, 0), pipeline_mode=pl.Buffered(1)),
            pl.BlockSpec((1, 1, cap, 1), lambda s, e, f: (s, e, 0, 0)),
            pl.BlockSpec((1, 8, d), lambda s, e, f: (s, 0, 0)),
            pl.BlockSpec((1, d), lambda s, e, f: (0, 0)),
            pl.BlockSpec((1, d, tf), lambda s, e, f: (e, 0, f)),
            pl.BlockSpec((1, d, tf), lambda s, e, f: (e, 0, f)),
            pl.BlockSpec((1, tf, d), lambda s, e, f: (e, f, 0)),
        ],
        out_specs=pl.BlockSpec((1, ts, d), lambda s, e, f: (s, 0, 0), pipeline_mode=pl.Buffered(1)),
        out_shape=jax.ShapeDtypeStruct((ns, ts, d), F32),
        scratch_shapes=[pltpu.VMEM((cap, d), BF16), pltpu.VMEM((cap, d), F32)],
        compiler_params=_cparams("parallel", "arbitrary", "arbitrary"),
    )(idx.reshape(ns, ne, 1, cap), x3, gates.reshape(ns, ne, cap, 1), mod, g_ffn, w_gate, w_up, w_down)


def _route(aff, *, n_dec, ts, batch, seq, ne):
    cap_dec = EC_CAPACITY * ts // ne
    cap_ctx = EC_CAPACITY * seq // ne
    a = aff[:, :ne]
    a_dec = a[:n_dec * ts].reshape(n_dec, ts, ne).swapaxes(1, 2)
    g_dec, i_dec = lax.top_k(a_dec, cap_dec)
    a_ctx = a[n_dec * ts:].reshape(batch, seq, ne).swapaxes(1, 2)
    g_ctx, i_ctx = lax.top_k(a_ctx, cap_ctx)
    i_ctx = i_ctx + (jnp.arange(batch) * seq)[:, None, None]
    g_ctx = g_ctx.swapaxes(0, 1).reshape(1, ne, batch * cap_ctx)
    i_ctx = i_ctx.swapaxes(0, 1).reshape(1, ne, batch * cap_ctx)
    return (jnp.concatenate([i_dec, i_ctx], axis=0).astype(jnp.int32),
            jnp.concatenate([g_dec, g_ctx], axis=0))


def _final_kernel(x_ref, g_ref, o_ref):
    x = x_ref[...]
    ms = jnp.mean(x * x, axis=-1, keepdims=True)
    o_ref[...] = (x * lax.rsqrt(ms + RMS_EPS)) * g_ref[...]


def _final_norm(x, gain, *, row0, nrows, tm):
    d = x.shape[1]
    rb0 = row0 // tm
    return pl.pallas_call(
        _final_kernel,
        name="final_norm",
        grid=(nrows // tm,),
        in_specs=[pl.BlockSpec((tm, d), lambda i: (rb0 + i, 0)),
                  pl.BlockSpec((1, d), lambda i: (0, 0))],
        out_specs=pl.BlockSpec((tm, d), lambda i: (i, 0)),
        out_shape=jax.ShapeDtypeStruct((nrows, d), F32),
        compiler_params=_cparams("parallel"),
    )(x, gain)


def kernel(x_prompt, x_sample, c, c_ctx, cache_attn_k, cache_attn_v, state_s5, state_gla, norm_mix, norm_ffn, w_mod, b_mod, w_router, ffn_w_gate, ffn_w_up, ffn_w_down, ab_w_in, ab_w_out, attn_q_norm, attn_k_norm, s5_lambda_re, s5_lambda_im, s5_log_dt, s5_b_re, s5_b_im, s5_c_re, s5_c_im, s5_d, s5_w_glu, s5_b_glu, gla_w_in, gla_w_gate_up, gla_b_gate, gla_o_norm, gla_w_out, norm_out):
    batch, seq, d = x_prompt.shape
    n_dec, ts, _ = x_sample.shape
    assert batch * seq == ts, "context batch must flatten to one latent-length sequence"
    depth = w_mod.shape[0]
    assert depth == 2 and ab_w_in.shape[0] == 1 and gla_w_in.shape[0] == 1
    ns = n_dec + 1
    ntok = ns * ts
    n_dec_tok = n_dec * ts
    ne = w_router.shape[2]
    tm = min(TOKEN_TILE, seq)
    tps = ts // tm
    qw = ATTN_HEADS * HEAD_DIM
    kw = ATTN_KV_HEADS * HEAD_DIM
    s5w = d // 2
    s5g = s5w // S5_GROUP_CH
    s5n = s5_lambda_re.shape[-1]
    past = cache_attn_k.shape[2]

    x = jnp.concatenate([x_sample.reshape(n_dec_tok, d), x_prompt.reshape(batch * seq, d)], axis=0)

    rows = -(-ns // 8) * 8
    cond = jnp.concatenate([c, c_ctx[None, :], jnp.zeros((rows - ns, d), F32)], axis=0)
    mod = _modulation(cond, w_mod, b_mod)[:, :ns].reshape(depth, ns, 6, d)
    mod = jnp.pad(mod, ((0, 0), (0, 0), (0, 2), (0, 0)))

    wr_pad = jnp.pad(w_router, ((0, 0), (0, 0), (0, LANES - ne)))
    wg_bf = ffn_w_gate.astype(BF16)
    wu_bf = ffn_w_up.astype(BF16)
    wd_bf = ffn_w_down.astype(BF16)
    tf = min(1024, ffn_w_gate.shape[-1])

    def moe_layer(xin, aff, layer):
        idx, gates = _route(aff, n_dec=n_dec, ts=ts, batch=batch, seq=seq, ne=ne)
        out = _moe(xin.reshape(ns, ts, d), idx, gates, mod[layer], norm_ffn[layer:layer + 1],
                   wg_bf[layer], wu_bf[layer], wd_bf[layer], tf=tf)
        return out.reshape(ntok, d)

    proj = _inproj_ab(x, mod[0], norm_mix[0:1], ab_w_in[0].astype(BF16), tps=tps, tm=tm)
    cos_tab, sin_tab = _rope_tables(ts)
    q_bf, k_bf, k_nrm, v_bf = _qkprep(proj, cos_tab, sin_tab, attn_q_norm[0], attn_k_norm[0],
                                      ts=ts, tm=tm, n_rope_seq=n_dec)
    keys = jnp.concatenate([cache_attn_k[:, 0].reshape(n_dec, past, kw).astype(BF16),
                            k_bf[:n_dec_tok].reshape(n_dec, ts, kw)], axis=1)
    vals = jnp.concatenate([cache_attn_v[:, 0].reshape(n_dec, past, kw).astype(BF16),
                            v_bf[:n_dec_tok].reshape(n_dec, ts, kw)], axis=1)
    o_attn = _attention(q_bf, keys.transpose(0, 2, 1), vals, None, row0=0, t=ts, tq=min(256, ts))
    o_attn = _attention(q_bf, k_bf[n_dec_tok:].reshape(batch, seq, kw).transpose(0, 2, 1),
                        v_bf[n_dec_tok:].reshape(batch, seq, kw), o_attn,
                        row0=n_dec_tok, t=seq, tq=min(256, seq))

    u0 = qw + 2 * kw
    mats = _s5_matrices(s5_lambda_re[0], s5_lambda_im[0], s5_log_dt[0], s5_b_re[0], s5_b_im[0],
                        s5_c_re[0], s5_c_im[0], s5_d[0])
    h0 = state_s5[:, 0].transpose(2, 1, 4, 0, 3).reshape(s5g, 4, n_dec, s5n)
    h0 = jnp.pad(h0, ((0, 0), (0, 0), (0, 0), (0, LANES - s5n)))
    y_dec, _ = _s5(_s5_to_chunks(proj[:n_dec_tok, u0:], n_dec, ts, s5g), *mats, h0,
                   nb=n_dec, nc=ts // S5_CHUNK)
    y_ctx, hfin = _s5(_s5_to_chunks(proj[n_dec_tok:, u0:], batch, seq, s5g), *mats,
                      jnp.zeros((s5g, 4, batch, LANES), F32), nb=batch, nc=seq // S5_CHUNK)
    y = jnp.concatenate([_s5_from_chunks(y_dec, n_dec, ts), _s5_from_chunks(y_ctx, batch, seq)], axis=0)

    w_out = ab_w_out[0].astype(BF16)
    x, aff = _ab_post(y, o_attn, x, mod[0], s5_w_glu[0].astype(BF16), s5_b_glu[0][None, :],
                      w_out[:qw], w_out[qw:], norm_ffn[0:1], wr_pad[0], tps=tps, tm=tm, ne=ne)
    x = moe_layer(x, aff, 0)

    nh = GLA_HEADS
    dk = d // 2 // nh
    dv = d // nh
    main_w = 2 * nh * dk + 2 * nh * dv
    w_gla = gla_w_in[0]
    wl = jnp.pad(w_gla[:, main_w:], ((0, 0), (0, LANES - 2 * GLA_RANK))).astype(BF16)
    wgu = jnp.zeros((LANES, 2 * nh * dk), F32)
    wgu = wgu.at[:GLA_RANK, :nh * dk].set(gla_w_gate_up[0, 0])
    wgu = wgu.at[GLA_RANK:2 * GLA_RANK, nh * dk:].set(gla_w_gate_up[0, 1])
    bgu = jnp.concatenate([gla_b_gate[0, 0], gla_b_gate[0, 1]])[None, :]
    proj1, gates1 = _inproj_gla(x, mod[1], norm_mix[1:2], w_gla[:, :main_w].astype(BF16), wl, wgu, bgu,
                                tps=tps, tm=tm)
    s0t = state_gla[:, 0].swapaxes(-1, -2)
    o_dirs, _ = _gla(proj1, gates1, s0t, None, row0=0, nb=n_dec, t=ts, tb=min(512, ts), dk=dk, dv=dv)
    o_dirs, sfin = _gla(proj1, gates1, jnp.zeros((batch, 2, nh, dv, dk), F32), o_dirs,
                        row0=n_dec_tok, nb=batch, t=seq, tb=min(512, seq), dk=dk, dv=dv)
    x, aff = _gla_post(o_dirs, proj1, x, mod[1], gla_o_norm[0][None, :], gla_w_out[0].astype(BF16),
                       norm_ffn[1:2], wr_pad[1], tps=tps, tm=tm, r_col=2 * nh * dk + nh * dv, ne=ne)
    x = moe_layer(x, aff, 1)

    gain = norm_out[None, :]
    y_sample = _final_norm(x, gain, row0=0, nrows=n_dec_tok, tm=tm).reshape(n_dec, ts, d)
    y_prompt = _final_norm(x, gain, row0=n_dec_tok, nrows=batch * seq, tm=tm).reshape(batch, seq, d)
    new_k = k_nrm[n_dec_tok:].reshape(batch, 1, seq, ATTN_KV_HEADS, HEAD_DIM)
    new_v = proj[n_dec_tok:, qw + kw:qw + 2 * kw].reshape(batch, 1, seq, ATTN_KV_HEADS, HEAD_DIM)
    new_s5 = hfin[..., :s5n].reshape(s5g, 2, 2, batch, s5n).transpose(3, 1, 0, 4, 2)[:, None]
    new_gla = sfin.swapaxes(-1, -2)[:, None]
    return (y_prompt, y_sample, new_k, new_v, new_s5, new_gla)
```

```python
import functools
import math

import jax
import jax.numpy as jnp
from jax import lax
from jax.experimental import pallas as pl
from jax.experimental.pallas import tpu as pltpu

F32 = jnp.float32
BF16 = jnp.bfloat16

RMS_EPS = 1e-6
GRID_W = 64
ROPE_THETA = 10000.0
ATTN_HEADS = 8
ATTN_KV_HEADS = 2
HEAD_DIM = 64
S5_GROUP_CH = 16
S5_CHUNK = 16
GLA_HEADS = 4
GLA_RANK = 16
GLA_GATE_NORM = 16.0
GLA_CHUNK = 64
EC_CAPACITY = 2
MOE_F_SUB = 256
MOE_F_PARTS = 2
LANES = 128
TOKEN_TILE = 512
VMEM_LIMIT = 56 * 1024 * 1024


def _cparams(*sem):
    return pltpu.CompilerParams(dimension_semantics=sem, vmem_limit_bytes=VMEM_LIMIT)


def _dot(a, b):
    return jnp.dot(a, b, preferred_element_type=F32)


def _dot_nt(a, b):
    return lax.dot_general(a, b, (((1,), (1,)), ((), ())), preferred_element_type=F32)


def _split(a):
    hi = a.astype(BF16)
    lo = (a - hi.astype(F32)).astype(BF16)
    return hi, lo


def _dot3(a, b):
    ah, al = _split(a)
    bh, bl = _split(b)
    return _dot(ah, bh) + _dot(al, bh) + _dot(ah, bl)


def _dot2(a, b_exact):
    ah, al = _split(a)
    return _dot(ah, b_exact) + _dot(al, b_exact)


def _sigmoid(x):
    return 1.0 / (1.0 + jnp.exp(-x))


def _silu(x):
    return x * _sigmoid(x)


def _gelu_tanh(x):
    c = math.sqrt(2.0 / math.pi)
    return 0.5 * x * (1.0 + jnp.tanh(c * (x + 0.044715 * (x * x * x))))


def _log_sigmoid(z):
    return jnp.minimum(z, 0.0) - jnp.log1p(jnp.exp(-jnp.abs(z)))


def _rms_mod(x, gain, scale, shift):
    ms = jnp.mean(x * x, axis=-1, keepdims=True)
    return (x * lax.rsqrt(ms + RMS_EPS)) * gain * (1.0 + scale) + shift


def _router_probs(xn, mod, gain, wr, ne):
    h = _rms_mod(xn, gain, mod[4:5], mod[3:4])
    logits = _dot3(h, wr)
    lane = lax.broadcasted_iota(jnp.int32, logits.shape, 1)
    logits = jnp.where(lane < ne, logits, -jnp.inf)
    m = jnp.max(logits, axis=-1, keepdims=True)
    p = jnp.exp(logits - m)
    return p / jnp.sum(p, axis=-1, keepdims=True)


def _mod_kernel(c_ref, w_ref, b_ref, o_ref):
    c = c_ref[...]
    o_ref[0] = _dot3(_silu(c), w_ref[0]) + b_ref[0]


def _modulation(cond, w_mod, b_mod):
    depth, d, n = w_mod.shape
    r = cond.shape[0]
    tn = n // 4
    return pl.pallas_call(
        _mod_kernel,
        name="modulation",
        grid=(depth, n // tn),
        in_specs=[
            pl.BlockSpec((r, d), lambda l, j: (0, 0)),
            pl.BlockSpec((1, d, tn), lambda l, j: (l, 0, j)),
            pl.BlockSpec((1, 1, tn), lambda l, j: (l, 0, j)),
        ],
        out_specs=pl.BlockSpec((1, r, tn), lambda l, j: (l, 0, j)),
        out_shape=jax.ShapeDtypeStruct((depth, r, n), F32),
        compiler_params=_cparams("parallel", "parallel"),
    )(cond, w_mod, b_mod.reshape(depth, 1, n))


def _inproj_ab_kernel(x_ref, mod_ref, g_ref, w_ref, o_ref, *, nsplit):
    m = mod_ref[0]
    h = _rms_mod(x_ref[...], g_ref[...], m[1:2], m[0:1]).astype(BF16)
    n = o_ref.shape[1]
    w = n // nsplit
    for s in range(nsplit):
        o_ref[:, s * w:(s + 1) * w] = _dot(h, w_ref[:, s * w:(s + 1) * w])


def _inproj_ab(x, mod, gain, w, *, tps, tm):
    ntok, d = x.shape
    n = w.shape[1]
    return pl.pallas_call(
        functools.partial(_inproj_ab_kernel, nsplit=2),
        name="inproj_ab",
        grid=(ntok // tm,),
        in_specs=[
            pl.BlockSpec((tm, d), lambda i: (i, 0)),
            pl.BlockSpec((1, 8, d), lambda i: (i // tps, 0, 0)),
            pl.BlockSpec((1, d), lambda i: (0, 0)),
            pl.BlockSpec((d, n), lambda i: (0, 0)),
        ],
        out_specs=pl.BlockSpec((tm, n), lambda i: (i, 0)),
        out_shape=jax.ShapeDtypeStruct((ntok, n), F32),
        compiler_params=_cparams("parallel"),
    )(x, mod, gain, w)


def _inproj_gla_kernel(x_ref, mod_ref, g_ref, w_ref, wl_ref, wg_ref, bg_ref, o_ref, gate_ref, *, nsplit):
    m = mod_ref[0]
    h = _rms_mod(x_ref[...], g_ref[...], m[1:2], m[0:1]).astype(BF16)
    n = o_ref.shape[1]
    w = n // nsplit
    for s in range(nsplit):
        o_ref[:, s * w:(s + 1) * w] = _dot(h, w_ref[:, s * w:(s + 1) * w])
    low = _dot(h, wl_ref[...])
    z = _dot3(low, wg_ref[...]) + bg_ref[...]
    gate_ref[...] = _log_sigmoid(z) / GLA_GATE_NORM


def _inproj_gla(x, mod, gain, w, wl, wg, bg, *, tps, tm):
    ntok, d = x.shape
    n = w.shape[1]
    ng = wg.shape[1]
    return pl.pallas_call(
        functools.partial(_inproj_gla_kernel, nsplit=n // 512),
        name="inproj_gla",
        grid=(ntok // tm,),
        in_specs=[
            pl.BlockSpec((tm, d), lambda i: (i, 0)),
            pl.BlockSpec((1, 8, d), lambda i: (i // tps, 0, 0)),
            pl.BlockSpec((1, d), lambda i: (0, 0)),
            pl.BlockSpec((d, n), lambda i: (0, 0)),
            pl.BlockSpec((d, LANES), lambda i: (0, 0)),
            pl.BlockSpec((LANES, ng), lambda i: (0, 0)),
            pl.BlockSpec((1, ng), lambda i: (0, 0)),
        ],
        out_specs=[pl.BlockSpec((tm, n), lambda i: (i, 0)),
                   pl.BlockSpec((tm, ng), lambda i: (i, 0))],
        out_shape=[jax.ShapeDtypeStruct((ntok, n), F32),
                   jax.ShapeDtypeStruct((ntok, ng), F32)],
        compiler_params=_cparams("parallel"),
    )(x, mod, gain, w, wl, wg, bg)


def _qkprep_kernel(q_ref, k_ref, v_ref, cos_ref, sin_ref, qg_ref, kg_ref, oq_ref, ok_ref,
                   qo_ref, ko_ref, kn_ref, vo_ref):
    cos = cos_ref[0]
    sin = sin_ref[0]

    def prep(x, gain, ones):
        width = x.shape[1]
        ssq = _dot2(x * x, ones)
        xn = x * lax.rsqrt(ssq * (1.0 / HEAD_DIM) + RMS_EPS) * gain
        lane = lax.broadcasted_iota(jnp.int32, x.shape, 1)
        first = (lane & 16) == 0
        partner = jnp.where(first, pltpu.roll(xn, width - 16, 1), pltpu.roll(xn, 16, 1))
        rep = width // LANES
        cosw = jnp.concatenate([cos] * rep, axis=1) if rep > 1 else cos
        sinw = jnp.concatenate([sin] * rep, axis=1) if rep > 1 else sin
        return xn, xn * cosw + partner * sinw

    _, qr = prep(q_ref[...], qg_ref[...], oq_ref[...])
    qo_ref[...] = (qr * (HEAD_DIM ** -0.5)).astype(BF16)
    kn, kr = prep(k_ref[...], kg_ref[...], ok_ref[...])
    kn_ref[...] = kn
    ko_ref[...] = kr.astype(BF16)
    vo_ref[...] = v_ref[...].astype(BF16)


def _qkprep(proj, cos_tab, sin_tab, q_gain, k_gain, *, ts, tm, n_rope_seq):
    ntok = proj.shape[0]
    qw = ATTN_HEADS * HEAD_DIM
    kw = ATTN_KV_HEADS * HEAD_DIM
    tps = ts // tm

    def head_ones(width):
        a = jnp.arange(width) // HEAD_DIM
        return (a[:, None] == a[None, :]).astype(BF16)

    def tab_map(i):
        return (jnp.minimum(i // tps, n_rope_seq) // n_rope_seq, i % tps, 0)

    return pl.pallas_call(
        _qkprep_kernel,
        name="qkprep",
        grid=(ntok // tm,),
        in_specs=[
            pl.BlockSpec((tm, qw), lambda i: (i, 0)),
            pl.BlockSpec((tm, kw), lambda i: (i, qw // kw)),
            pl.BlockSpec((tm, kw), lambda i: (i, qw // kw + 1)),
            pl.BlockSpec((1, tm, LANES), tab_map),
            pl.BlockSpec((1, tm, LANES), tab_map),
            pl.BlockSpec((1, qw), lambda i: (0, 0)),
            pl.BlockSpec((1, kw), lambda i: (0, 0)),
            pl.BlockSpec((qw, qw), lambda i: (0, 0)),
            pl.BlockSpec((kw, kw), lambda i: (0, 0)),
        ],
        out_specs=[
            pl.BlockSpec((tm, qw), lambda i: (i, 0)),
            pl.BlockSpec((tm, kw), lambda i: (i, 0)),
            pl.BlockSpec((tm, kw), lambda i: (i, 0)),
            pl.BlockSpec((tm, kw), lambda i: (i, 0)),
        ],
        out_shape=[
            jax.ShapeDtypeStruct((ntok, qw), BF16),
            jax.ShapeDtypeStruct((ntok, kw), BF16),
            jax.ShapeDtypeStruct((ntok, kw), F32),
            jax.ShapeDtypeStruct((ntok, kw), BF16),
        ],
        compiler_params=_cparams("parallel"),
    )(proj, proj, proj, cos_tab, sin_tab,
      jnp.tile(q_gain, ATTN_HEADS).reshape(1, qw), jnp.tile(k_gain, ATTN_KV_HEADS).reshape(1, kw),
      head_ones(qw), head_ones(kw))


def _rope_tables(t):
    nfreq = HEAD_DIM // 4
    inv = ROPE_THETA ** (-jnp.arange(nfreq, dtype=F32) / nfreq)
    pos = jnp.arange(t)
    row = (pos // GRID_W).astype(F32)[:, None] * inv[None, :]
    col = (pos % GRID_W).astype(F32)[:, None] * inv[None, :]
    cos = jnp.concatenate([jnp.cos(row), jnp.cos(row), jnp.cos(col), jnp.cos(col)], axis=1)
    sin = jnp.concatenate([-jnp.sin(row), jnp.sin(row), -jnp.sin(col), jnp.sin(col)], axis=1)
    rep = LANES // HEAD_DIM
    cos = jnp.tile(cos, (1, rep))
    sin = jnp.tile(sin, (1, rep))
    return (jnp.stack([cos, jnp.ones_like(cos)]), jnp.stack([sin, jnp.zeros_like(sin)]))


def _attn_kernel(q_ref, kt_ref, v_ref, *rest):
    o_ref = rest[-1]
    q = q_ref[...]
    v = v_ref[0]
    group = ATTN_HEADS // ATTN_KV_HEADS
    outs = []
    for h in range(ATTN_HEADS):
        kv = h // group
        qh = q[:, h * HEAD_DIM:(h + 1) * HEAD_DIM]
        kt = kt_ref[0, kv * HEAD_DIM:(kv + 1) * HEAD_DIM, :]
        s = _dot(qh, kt)
        p = jnp.exp(s - jnp.max(s, axis=-1, keepdims=True))
        l = jnp.sum(p, axis=-1, keepdims=True)
        o = _dot(p.astype(BF16), v)
        outs.append(o[:, kv * HEAD_DIM:(kv + 1) * HEAD_DIM] / l)
    o_ref[...] = jnp.concatenate(outs, axis=1).astype(BF16)


def _attention(q, keys_t, vals, prev, *, row0, t, tq):
    b, kw, s = keys_t.shape
    ntok, qw = q.shape
    nq = t // tq
    rb0 = row0 // tq
    extra = [] if prev is None else [prev]
    return pl.pallas_call(
        _attn_kernel,
        name="attention",
        grid=(b, nq),
        in_specs=[
            pl.BlockSpec((tq, qw), lambda bi, i: (rb0 + bi * nq + i, 0)),
            pl.BlockSpec((1, kw, s), lambda bi, i: (bi, 0, 0)),
            pl.BlockSpec((1, s, kw), lambda bi, i: (bi, 0, 0)),
        ] + [pl.BlockSpec(memory_space=pl.ANY)] * len(extra),
        out_specs=pl.BlockSpec((tq, qw), lambda bi, i: (rb0 + bi * nq + i, 0)),
        out_shape=jax.ShapeDtypeStruct((ntok, qw), BF16),
        input_output_aliases={3: 0} if extra else {},
        compiler_params=_cparams("parallel", "parallel"),
    )(q, keys_t, vals, *extra)


def _s5_kernel(x_ref, e_ref, m_ref, f_ref, lam_ref, h0_ref, y_ref, hfin_ref, s_ref, hin_ref, *, nb, nc):
    x = x_ref[0]
    s_ref[...] = _dot3(x, e_ref[0])
    lam = lam_ref[0]
    lfr = jnp.broadcast_to(lam[0:1], (nb, LANES))
    lfi = jnp.broadcast_to(lam[1:2], (nb, LANES))
    lbr = jnp.broadcast_to(lam[2:3], (nb, LANES))
    lbi = jnp.broadcast_to(lam[3:4], (nb, LANES))

    def step(c, carry):
        hfr, hfi, hbr, hbi = carry
        rf = pl.multiple_of(c * nb, nb)
        rb = pl.multiple_of((nc - 1 - c) * nb, nb)
        hin_ref[pl.ds(rf, nb), 0:LANES] = hfr
        hin_ref[pl.ds(rf, nb), LANES:2 * LANES] = hfi
        hin_ref[pl.ds(rb, nb), 2 * LANES:3 * LANES] = hbr
        hin_ref[pl.ds(rb, nb), 3 * LANES:4 * LANES] = hbi
        sfr = s_ref[pl.ds(rf, nb), 0:LANES]
        sfi = s_ref[pl.ds(rf, nb), LANES:2 * LANES]
        sbr = s_ref[pl.ds(rb, nb), 2 * LANES:3 * LANES]
        sbi = s_ref[pl.ds(rb, nb), 3 * LANES:4 * LANES]
        return (lfr * hfr - lfi * hfi + sfr, lfr * hfi + lfi * hfr + sfi,
                lbr * hbr - lbi * hbi + sbr, lbr * hbi + lbi * hbr + sbi)

    h0 = h0_ref[0]
    hfr, hfi, hbr, hbi = lax.fori_loop(0, nc, step, (h0[0], h0[1], h0[2], h0[3]))
    hfin_ref[0, 0] = hfr
    hfin_ref[0, 1] = hfi
    hfin_ref[0, 2] = hbr
    hfin_ref[0, 3] = hbi
    y_ref[0] = _dot3(x, m_ref[0]) + _dot3(hin_ref[...], f_ref[0])


def _s5(xg, e_mat, m_mat, f_mat, lam_l, h0, *, nb, nc):
    g, r, w = xg.shape
    sw = 4 * LANES
    return pl.pallas_call(
        functools.partial(_s5_kernel, nb=nb, nc=nc),
        name="s5",
        grid=(g,),
        in_specs=[
            pl.BlockSpec((1, r, w), lambda i: (i, 0, 0)),
            pl.BlockSpec((1, w, sw), lambda i: (i, 0, 0)),
            pl.BlockSpec((1, w, w), lambda i: (i, 0, 0)),
            pl.BlockSpec((1, sw, w), lambda i: (i, 0, 0)),
            pl.BlockSpec((1, 8, LANES), lambda i: (i, 0, 0)),
            pl.BlockSpec((1, 4, nb, LANES), lambda i: (i, 0, 0, 0)),
        ],
        out_specs=[pl.BlockSpec((1, r, w), lambda i: (i, 0, 0)),
                   pl.BlockSpec((1, 4, nb, LANES), lambda i: (i, 0, 0, 0))],
        out_shape=[jax.ShapeDtypeStruct((g, r, w), F32),
                   jax.ShapeDtypeStruct((g, 4, nb, LANES), F32)],
        scratch_shapes=[pltpu.VMEM((r, sw), F32), pltpu.VMEM((r, sw), F32)],
        compiler_params=_cparams("parallel"),
    )(xg, e_mat, m_mat, f_mat, lam_l, h0)


def _s5_matrices(lam_re, lam_im, log_dt, b_re, b_im, c_re, c_im, d_skip):
    hi = lax.Precision.HIGHEST
    L = S5_CHUNK
    _, g, n = lam_re.shape
    p = b_re.shape[-1]
    dt = jnp.exp(log_dt)[..., None]
    mag = jnp.exp(lam_re * dt)
    lbr, lbi = mag * jnp.cos(lam_im * dt), mag * jnp.sin(lam_im * dt)
    den = lam_re * lam_re + lam_im * lam_im
    cr = ((lbr - 1.0) * lam_re + lbi * lam_im) / den
    ci = (lbi * lam_re - (lbr - 1.0) * lam_im) / den
    bbr = cr[..., None] * b_re - ci[..., None] * b_im
    bbi = cr[..., None] * b_im + ci[..., None] * b_re
    pr, pi = [jnp.ones_like(lbr)], [jnp.zeros_like(lbr)]
    for _ in range(L):
        pr, pi = pr + [pr[-1] * lbr - pi[-1] * lbi], pi + [pr[-1] * lbi + pi[-1] * lbr]
    pwr, pwi = jnp.stack(pr, axis=2), jnp.stack(pi, axis=2)
    cpr = c_re[:, :, None] * pwr[:, :, :, None, :] - c_im[:, :, None] * pwi[:, :, :, None, :]
    cpi = c_re[:, :, None] * pwi[:, :, :, None, :] + c_im[:, :, None] * pwr[:, :, :, None, :]
    kern = (jnp.einsum('dgkpn,dgnq->dgkpq', cpr[:, :, :L], bbr, precision=hi)
            - jnp.einsum('dgkpn,dgnq->dgkpq', cpi[:, :, :L], bbi, precision=hi))
    step = jnp.arange(L)
    lag_f = step[None, :] - step[:, None]
    lag_b = -lag_f

    def toeplitz(kd, lag):
        t = jnp.take(kd, jnp.clip(lag, 0, L - 1), axis=1)
        t = jnp.where((lag >= 0)[None, :, :, None, None], t, 0.0)
        return t.transpose(0, 1, 4, 2, 3).reshape(g, L * p, L * p)

    eye_l = jnp.eye(L, dtype=F32)
    dmat = jnp.einsum('ji,gq,qp->gjqip', eye_l, d_skip, jnp.eye(p, dtype=F32)).reshape(g, L * p, L * p)
    m_mat = toeplitz(kern[0], lag_f) + toeplitz(kern[1], lag_b) + dmat

    def e_block(pw_r, pw_i, b_r, b_i):
        er = pw_r[:, :, None, :] * b_r.transpose(0, 2, 1)[:, None] - pw_i[:, :, None, :] * b_i.transpose(0, 2, 1)[:, None]
        ei = pw_r[:, :, None, :] * b_i.transpose(0, 2, 1)[:, None] + pw_i[:, :, None, :] * b_r.transpose(0, 2, 1)[:, None]
        return er.reshape(g, L * p, n), ei.reshape(g, L * p, n)

    def pad_cols(a):
        return jnp.pad(a, ((0, 0), (0, 0), (0, LANES - n)))

    efr, efi = e_block(pwr[0][:, L - 1 - step], pwi[0][:, L - 1 - step], bbr[0], bbi[0])
    ebr, ebi = e_block(pwr[1][:, step], pwi[1][:, step], bbr[1], bbi[1])
    e_mat = jnp.concatenate([pad_cols(efr), pad_cols(efi), pad_cols(ebr), pad_cols(ebi)], axis=2)

    def f_block(c_r, c_i):
        fr = c_r.transpose(0, 3, 1, 2).reshape(g, n, L * p)
        fi = -c_i.transpose(0, 3, 1, 2).reshape(g, n, L * p)
        return fr, fi

    def pad_rows(a):
        return jnp.pad(a, ((0, 0), (0, LANES - n), (0, 0)))

    ffr, ffi = f_block(cpr[0][:, step + 1], cpi[0][:, step + 1])
    fbr, fbi = f_block(cpr[1][:, L - step], cpi[1][:, L - step])
    f_mat = jnp.concatenate([pad_rows(ffr), pad_rows(ffi), pad_rows(fbr), pad_rows(fbi)], axis=1)

    lam_l = jnp.stack([pwr[0, :, L], pwi[0, :, L], pwr[1, :, L], pwi[1, :, L]], axis=1)
    lam_l = jnp.pad(lam_l, ((0, 0), (0, 4), (0, LANES - n)))
    return e_mat, m_mat, f_mat, lam_l


def _s5_to_chunks(u, nb, t, g):
    p = u.shape[1] // g
    nc = t // S5_CHUNK
    x = u.reshape(nb, nc, S5_CHUNK, g, p).transpose(3, 1, 0, 2, 4)
    return x.reshape(g, nc * nb, S5_CHUNK * p)


def _s5_from_chunks(y, nb, t):
    g, _, w = y.shape
    p = w // S5_CHUNK
    nc = t // S5_CHUNK
    return y.reshape(g, nc, nb, S5_CHUNK, p).transpose(2, 1, 3, 0, 4).reshape(nb * t, g * p)


def _ab_post_kernel(y_ref, oa_ref, x_ref, mod_ref, wglu_ref, bglu_ref, woa_ref, wos_ref, gffn_ref, wr_ref,
                    xo_ref, aff_ref, *, ne):
    m = mod_ref[0]
    half = wos_ref.shape[0]
    glu = _dot(_gelu_tanh(y_ref[...]).astype(BF16), wglu_ref[...]) + bglu_ref[...]
    o_s5 = glu[:, :half] * _sigmoid(glu[:, half:])
    out = _dot(oa_ref[...], woa_ref[...]) + _dot(o_s5.astype(BF16), wos_ref[...])
    xn = x_ref[...] + m[2:3] * out
    xo_ref[...] = xn
    aff_ref[...] = _router_probs(xn, m, gffn_ref[...], wr_ref[...], ne)


def _ab_post(y, o_attn, x, mod, w_glu, b_glu, w_out_a, w_out_s, g_ffn, w_router, *, tps, tm, ne):
    ntok, d = x.shape
    hw = y.shape[1]
    full = lambda shape: pl.BlockSpec(shape, lambda i: (0,) * len(shape))
    return pl.pallas_call(
        functools.partial(_ab_post_kernel, ne=ne),
        name="ab_post",
        grid=(ntok // tm,),
        in_specs=[
            pl.BlockSpec((tm, hw), lambda i: (i, 0)),
            pl.BlockSpec((tm, o_attn.shape[1]), lambda i: (i, 0)),
            pl.BlockSpec((tm, d), lambda i: (i, 0)),
            pl.BlockSpec((1, 8, d), lambda i: (i // tps, 0, 0)),
            full(w_glu.shape), full(b_glu.shape), full(w_out_a.shape), full(w_out_s.shape),
            full(g_ffn.shape), full(w_router.shape),
        ],
        out_specs=[pl.BlockSpec((tm, d), lambda i: (i, 0)),
                   pl.BlockSpec((tm, LANES), lambda i: (i, 0))],
        out_shape=[jax.ShapeDtypeStruct((ntok, d), F32),
                   jax.ShapeDtypeStruct((ntok, LANES), F32)],
        compiler_params=_cparams("parallel"),
    )(y, o_attn, x, mod, w_glu, b_glu, w_out_a, w_out_s, g_ffn, w_router)


def _gla_post_kernel(of_ref, ob_ref, r_ref, x_ref, mod_ref, gn_ref, wo_ref, gffn_ref, wr_ref, xo_ref, aff_ref,
                     *, ne):
    m = mod_ref[0]
    o = of_ref[...] + ob_ref[...]
    dv = gn_ref.shape[1]
    parts = []
    for h in range(GLA_HEADS):
        oh = o[:, h * dv:(h + 1) * dv]
        ms = jnp.mean(oh * oh, axis=-1, keepdims=True)
        parts.append(oh * lax.rsqrt(ms + RMS_EPS) * gn_ref[...])
    on = jnp.concatenate(parts, axis=1) * _silu(r_ref[...])
    out = _dot(on.astype(BF16), wo_ref[...])
    xn = x_ref[...] + m[2:3] * out
    xo_ref[...] = xn
    aff_ref[...] = _router_probs(xn, m, gffn_ref[...], wr_ref[...], ne)


def _gla_post(o_fwd, o_bwd, proj, x, mod, g_norm, w_out, g_ffn, w_router, *, tps, tm, r_col, ne):
    ntok, d = x.shape
    vw = o_fwd.shape[1]
    full = lambda shape: pl.BlockSpec(shape, lambda i: (0,) * len(shape))
    return pl.pallas_call(
        functools.partial(_gla_post_kernel, ne=ne),
        name="gla_post",
        grid=(ntok // tm,),
        in_specs=[
            pl.BlockSpec((tm, vw), lambda i: (i, 0)),
            pl.BlockSpec((tm, vw), lambda i: (i, 0)),
            pl.BlockSpec((tm, vw), lambda i: (i, r_col // vw)),
            pl.BlockSpec((tm, d), lambda i: (i, 0)),
            pl.BlockSpec((1, 8, d), lambda i: (i // tps, 0, 0)),
            full(g_norm.shape), full(w_out.shape), full(g_ffn.shape), full(w_router.shape),
        ],
        out_specs=[pl.BlockSpec((tm, d), lambda i: (i, 0)),
                   pl.BlockSpec((tm, LANES), lambda i: (i, 0))],
        out_shape=[jax.ShapeDtypeStruct((ntok, d), F32),
                   jax.ShapeDtypeStruct((ntok, LANES), F32)],
        compiler_params=_cparams("parallel"),
    )(o_fwd, o_bwd, proj, x, mod, g_norm, w_out, g_ffn, w_router)


def _gla_kernel(qf_ref, kf_ref, vf_ref, gf_ref, qb_ref, kb_ref, vb_ref, gb_ref, s0_ref, *rest,
                nchunk, dk, dv, nh):
    of_ref, ob_ref, sfin_ref, st_ref = rest[-4:]
    j = pl.program_id(1)
    c = GLA_CHUNK

    @pl.when(j == 0)
    def _():
        st_ref[...] = s0_ref[0]

    row = lax.broadcasted_iota(jnp.int32, (c, c), 0)
    col = lax.broadcasted_iota(jnp.int32, (c, c), 1)
    lower = row >= col
    upper = row <= col
    dirs = ((qf_ref, kf_ref, vf_ref, gf_ref, of_ref, lower, lower.astype(BF16)),
            (qb_ref, kb_ref, vb_ref, gb_ref, ob_ref, upper, upper.astype(BF16)))

    def chunk(d, h, r0):
        q_ref, k_ref, v_ref, g_ref, o_ref, tri, tri_b = dirs[d]
        ks = slice(h * dk, (h + 1) * dk)
        vs = slice(h * dv, (h + 1) * dv)
        qi = q_ref[pl.ds(r0, c), ks] * (dk ** -0.5)
        ki = k_ref[pl.ds(r0, c), ks]
        vi = v_ref[pl.ds(r0, c), vs]
        gi = g_ref[pl.ds(r0, c), ks]
        cum = _dot_tri(tri_b, gi)
        tot = jnp.sum(gi, axis=0, keepdims=True)
        q_dec = (qi * jnp.exp(cum)).astype(BF16)
        k_inv = (ki * jnp.exp(-cum)).astype(BF16)
        k_end = (ki * jnp.exp(tot - cum)).astype(BF16)
        scores = jnp.where(tri, _dot_nt(q_dec, k_inv), 0.0)
        st = st_ref[d, h]
        o = _dot(scores.astype(BF16), vi.astype(BF16)) + _dot_nt(q_dec, st.astype(BF16))
        o_ref[pl.ds(r0, c), vs] = o
        st_ref[d, h] = st * jnp.exp(tot) + _dot(vi.T.astype(BF16), k_end)

    def step(i, carry):
        rf = pl.multiple_of(i * c, c)
        rb = pl.multiple_of((nchunk - 1 - i) * c, c)
        for h in range(nh):
            chunk(0, h, rf)
            chunk(1, h, rb)
        return carry

    lax.fori_loop(0, nchunk, step, 0)

    @pl.when(j == pl.num_programs(1) - 1)
    def _():
        sfin_ref[0] = st_ref[...]


def _dot_tri(tri_b, g):
    gh, gl = _split(g)
    return _dot(tri_b, gh) + _dot(tri_b, gl)


def _gla(proj, gates, s0t, prev, *, row0, nb, t, tb, dk, dv):
    h = GLA_HEADS
    ntok = proj.shape[0]
    nblk = t // tb
    rb0 = row0 // tb
    kw, vw = h * dk, h * dv
    fwd = lambda b, j: rb0 + b * nblk + j
    bwd = lambda b, j: rb0 + b * nblk + nblk - 1 - j

    def views(rows, gate_col):
        return [pl.BlockSpec((tb, kw), lambda b, j: (rows(b, j), 0)),
                pl.BlockSpec((tb, kw), lambda b, j: (rows(b, j), 1)),
                pl.BlockSpec((tb, vw), lambda b, j: (rows(b, j), (2 * kw) // vw)),
                pl.BlockSpec((tb, kw), lambda b, j: (rows(b, j), gate_col))]

    extra = [] if prev is None else list(prev)
    state_spec = pl.BlockSpec((1, 2, h, dv, dk), lambda b, j: (b, 0, 0, 0, 0))
    return pl.pallas_call(
        functools.partial(_gla_kernel, nchunk=tb // GLA_CHUNK, dk=dk, dv=dv, nh=h),
        name="gla",
        input_output_aliases={9: 0, 10: 1} if extra else {},
        grid=(nb, nblk),
        in_specs=views(fwd, 0) + views(bwd, 1) + [state_spec] + [pl.BlockSpec(memory_space=pl.ANY)] * len(extra),
        out_specs=[
            pl.BlockSpec((tb, vw), lambda b, j: (fwd(b, j), 0)),
            pl.BlockSpec((tb, vw), lambda b, j: (bwd(b, j), 0)),
            state_spec,
        ],
        out_shape=[jax.ShapeDtypeStruct((ntok, vw), F32),
                   jax.ShapeDtypeStruct((ntok, vw), F32),
                   jax.ShapeDtypeStruct((nb, 2, h, dv, dk), F32)],
        scratch_shapes=[pltpu.VMEM((2, h, dv, dk), F32)],
        compiler_params=_cparams("parallel", "arbitrary"),
    )(proj, proj, proj, gates, proj, proj, proj, gates, s0t, *extra)


def _moe_kernel(idx_ref, gsm_ref, x_ref, mod_ref, gffn_ref, wg_ref, wu_ref, wd_ref, o_ref,
                xs_ref, gbuf_ref, ya_ref, yb_ref, *, cap, ne, nf):
    e = pl.program_id(1)
    f = pl.program_id(2)
    m = mod_ref[0]
    g2 = m[5:6]
    nsub = wg_ref.shape[1]
    rows_per = cap // (nf * nsub)
    e_next = jnp.minimum(e + 1, ne - 1)
    e_prev = jnp.maximum(e - 1, 0)

    def gather_row(en, r):
        gbuf_ref[pl.ds(r, 1), :] = x_ref[0, pl.ds(idx_ref[0, en, 0, r], 1), :]

    def scatter_row(ep, y_ref, r):
        i = idx_ref[0, ep, 0, r]
        o_ref[0, pl.ds(i, 1), :] += y_ref[pl.ds(r, 1), :] * gsm_ref[0, ep, 0, r] * g2

    @pl.when((e == 0) & (f == 0))
    def _():
        o_ref[...] = x_ref[...]
        yb_ref[...] = jnp.zeros(yb_ref.shape, F32)

        def body(r, carry):
            gather_row(0, r)
            return carry
        lax.fori_loop(0, cap, body, 0, unroll=8)

    @pl.when(f == 0)
    def _():
        xs_ref[...] = _rms_mod(gbuf_ref[...], gffn_ref[...], m[4:5], m[3:4]).astype(BF16)

    def expert_part(y_cur, y_prev):
        @pl.when(f == 0)
        def _():
            y_cur[...] = jnp.zeros(y_cur.shape, F32)

        def sub_chunk(k, carry):
            r0 = pl.multiple_of((f * nsub + k) * rows_per, rows_per)
            for r in range(rows_per):
                gather_row(e_next, r0 + r)
                scatter_row(e_prev, y_prev, r0 + r)
            xs = xs_ref[...]
            hid = _silu(_dot(xs, wg_ref[0, k])) * _dot(xs, wu_ref[0, k])
            y_cur[...] += _dot(hid.astype(BF16), wd_ref[0, k])
            return carry

        lax.fori_loop(0, nsub, sub_chunk, 0)

        @pl.when((e == ne - 1) & (f == nf - 1))
        def _():
            def body(r, carry):
                scatter_row(e, y_cur, r)
                return carry
            lax.fori_loop(0, cap, body, 0, unroll=8)

    @pl.when(e % 2 == 0)
    def _():
        expert_part(ya_ref, yb_ref)

    @pl.when(e % 2 == 1)
    def _():
        expert_part(yb_ref, ya_ref)


def _moe(x3, idx, gates, mod, g_ffn, w_gate, w_up, w_down, *, nf):
    ns, ts, d = x3.shape
    ne, nsub_all, _, sub = w_gate.shape
    cap = idx.shape[2]
    nsub = nsub_all // nf
    return pl.pallas_call(
        functools.partial(_moe_kernel, cap=cap, ne=ne, nf=nf),
        name="moe",
        grid=(ns, ne, nf),
        in_specs=[
            pl.BlockSpec((1, ne, 1, cap), lambda s, e, f: (s, 0, 0, 0), memory_space=pltpu.SMEM),
            pl.BlockSpec((1, ne, 1, cap), lambda s, e, f: (s, 0, 0, 0), memory_space=pltpu.SMEM),
            pl.BlockSpec((1, ts, d), lambda s, e, f: (s, 0, 0), pipeline_mode=pl.Buffered(1)),
            pl.BlockSpec((1, 8, d), lambda s, e, f: (s, 0, 0)),
            pl.BlockSpec((1, d), lambda s, e, f: (0, 0)),
            pl.BlockSpec((1, nsub, d, sub), lambda s, e, f: (e, f, 0, 0)),
            pl.BlockSpec((1, nsub, d, sub), lambda s, e, f: (e, f, 0, 0)),
            pl.BlockSpec((1, nsub, sub, d), lambda s, e, f: (e, f, 0, 0)),
        ],
        out_specs=pl.BlockSpec((1, ts, d), lambda s, e, f: (s, 0, 0), pipeline_mode=pl.Buffered(1)),
        out_shape=jax.ShapeDtypeStruct((ns, ts, d), F32),
        scratch_shapes=[pltpu.VMEM((cap, d), BF16), pltpu.VMEM((cap, d), F32),
                        pltpu.VMEM((cap, d), F32), pltpu.VMEM((cap, d), F32)],
        compiler_params=_cparams("parallel", "arbitrary", "arbitrary"),
    )(idx.reshape(ns, ne, 1, cap), gates.reshape(ns, ne, 1, cap), x3, mod, g_ffn, w_gate, w_up, w_down)


def _route(aff, *, n_dec, ts, batch, seq, ne):
    cap_dec = EC_CAPACITY * ts // ne
    cap_ctx = EC_CAPACITY * seq // ne
    a = aff[:, :ne]
    a_dec = a[:n_dec * ts].reshape(n_dec, ts, ne).swapaxes(1, 2)
    g_dec, i_dec = lax.top_k(a_dec, cap_dec)
    a_ctx = a[n_dec * ts:].reshape(batch, seq, ne).swapaxes(1, 2)
    g_ctx, i_ctx = lax.top_k(a_ctx, cap_ctx)
    i_ctx = i_ctx + (jnp.arange(batch) * seq)[:, None, None]
    g_ctx = g_ctx.swapaxes(0, 1).reshape(1, ne, batch * cap_ctx)
    i_ctx = i_ctx.swapaxes(0, 1).reshape(1, ne, batch * cap_ctx)
    return (jnp.concatenate([i_dec, i_ctx], axis=0).astype(jnp.int32),
            jnp.concatenate([g_dec, g_ctx], axis=0))


def _final_kernel(x_ref, g_ref, o_ref):
    x = x_ref[...]
    ms = jnp.mean(x * x, axis=-1, keepdims=True)
    o_ref[...] = (x * lax.rsqrt(ms + RMS_EPS)) * g_ref[...]


def _final_norm(x, gain, *, row0, nrows, tm):
    d = x.shape[1]
    rb0 = row0 // tm
    return pl.pallas_call(
        _final_kernel,
        name="final_norm",
        grid=(nrows // tm,),
        in_specs=[pl.BlockSpec((tm, d), lambda i: (rb0 + i, 0)),
                  pl.BlockSpec((1, d), lambda i: (0, 0))],
        out_specs=pl.BlockSpec((tm, d), lambda i: (i, 0)),
        out_shape=jax.ShapeDtypeStruct((nrows, d), F32),
        compiler_params=_cparams("parallel"),
    )(x, gain)


def kernel(x_prompt, x_sample, c, c_ctx, cache_attn_k, cache_attn_v, state_s5, state_gla, norm_mix, norm_ffn, w_mod, b_mod, w_router, ffn_w_gate, ffn_w_up, ffn_w_down, ab_w_in, ab_w_out, attn_q_norm, attn_k_norm, s5_lambda_re, s5_lambda_im, s5_log_dt, s5_b_re, s5_b_im, s5_c_re, s5_c_im, s5_d, s5_w_glu, s5_b_glu, gla_w_in, gla_w_gate_up, gla_b_gate, gla_o_norm, gla_w_out, norm_out):
    batch, seq, d = x_prompt.shape
    n_dec, ts, _ = x_sample.shape
    assert batch * seq == ts, "context batch must flatten to one latent-length sequence"
    depth = w_mod.shape[0]
    assert depth == 2 and ab_w_in.shape[0] == 1 and gla_w_in.shape[0] == 1
    ns = n_dec + 1
    ntok = ns * ts
    n_dec_tok = n_dec * ts
    ne = w_router.shape[2]
    tm = min(TOKEN_TILE, seq)
    tps = ts // tm
    qw = ATTN_HEADS * HEAD_DIM
    kw = ATTN_KV_HEADS * HEAD_DIM
    s5w = d // 2
    s5g = s5w // S5_GROUP_CH
    s5n = s5_lambda_re.shape[-1]
    past = cache_attn_k.shape[2]

    x = jnp.concatenate([x_sample.reshape(n_dec_tok, d), x_prompt.reshape(batch * seq, d)], axis=0)

    rows = -(-ns // 8) * 8
    cond = jnp.concatenate([c, c_ctx[None, :], jnp.zeros((rows - ns, d), F32)], axis=0)
    mod = _modulation(cond, w_mod, b_mod)[:, :ns].reshape(depth, ns, 6, d)
    mod = jnp.pad(mod, ((0, 0), (0, 0), (0, 2), (0, 0)))

    wr_pad = jnp.pad(w_router, ((0, 0), (0, 0), (0, LANES - ne)))
    ff = ffn_w_gate.shape[-1]
    nsub_all = ff // MOE_F_SUB
    wg_bf = ffn_w_gate.astype(BF16).reshape(depth, ne, d, nsub_all, MOE_F_SUB).transpose(0, 1, 3, 2, 4)
    wu_bf = ffn_w_up.astype(BF16).reshape(depth, ne, d, nsub_all, MOE_F_SUB).transpose(0, 1, 3, 2, 4)
    wd_bf = ffn_w_down.astype(BF16).reshape(depth, ne, nsub_all, MOE_F_SUB, d)

    def moe_layer(xin, aff, layer):
        idx, gates = _route(aff, n_dec=n_dec, ts=ts, batch=batch, seq=seq, ne=ne)
        out = _moe(xin.reshape(ns, ts, d), idx, gates, mod[layer], norm_ffn[layer:layer + 1],
                   wg_bf[layer], wu_bf[layer], wd_bf[layer], nf=MOE_F_PARTS)
        return out.reshape(ntok, d)

    proj = _inproj_ab(x, mod[0], norm_mix[0:1], ab_w_in[0].astype(BF16), tps=tps, tm=tm)
    cos_tab, sin_tab = _rope_tables(ts)
    q_bf, k_bf, k_nrm, v_bf = _qkprep(proj, cos_tab, sin_tab, attn_q_norm[0], attn_k_norm[0],
                                      ts=ts, tm=tm, n_rope_seq=n_dec)
    keys = jnp.concatenate([cache_attn_k[:, 0].reshape(n_dec, past, kw).astype(BF16),
                            k_bf[:n_dec_tok].reshape(n_dec, ts, kw)], axis=1)
    vals = jnp.concatenate([cache_attn_v[:, 0].reshape(n_dec, past, kw).astype(BF16),
                            v_bf[:n_dec_tok].reshape(n_dec, ts, kw)], axis=1)
    o_attn = _attention(q_bf, keys.transpose(0, 2, 1), vals, None, row0=0, t=ts, tq=min(256, ts))
    o_attn = _attention(q_bf, k_bf[n_dec_tok:].reshape(batch, seq, kw).transpose(0, 2, 1),
                        v_bf[n_dec_tok:].reshape(batch, seq, kw), o_attn,
                        row0=n_dec_tok, t=seq, tq=min(256, seq))

    u0 = qw + 2 * kw
    mats = _s5_matrices(s5_lambda_re[0], s5_lambda_im[0], s5_log_dt[0], s5_b_re[0], s5_b_im[0],
                        s5_c_re[0], s5_c_im[0], s5_d[0])
    h0 = state_s5[:, 0].transpose(2, 1, 4, 0, 3).reshape(s5g, 4, n_dec, s5n)
    h0 = jnp.pad(h0, ((0, 0), (0, 0), (0, 0), (0, LANES - s5n)))
    y_dec, _ = _s5(_s5_to_chunks(proj[:n_dec_tok, u0:], n_dec, ts, s5g), *mats, h0,
                   nb=n_dec, nc=ts // S5_CHUNK)
    y_ctx, hfin = _s5(_s5_to_chunks(proj[n_dec_tok:, u0:], batch, seq, s5g), *mats,
                      jnp.zeros((s5g, 4, batch, LANES), F32), nb=batch, nc=seq // S5_CHUNK)
    y = jnp.concatenate([_s5_from_chunks(y_dec, n_dec, ts), _s5_from_chunks(y_ctx, batch, seq)], axis=0)

    w_out = ab_w_out[0].astype(BF16)
    x, aff = _ab_post(y, o_attn, x, mod[0], s5_w_glu[0].astype(BF16), s5_b_glu[0][None, :],
                      w_out[:qw], w_out[qw:], norm_ffn[0:1], wr_pad[0], tps=tps, tm=tm, ne=ne)
    x = moe_layer(x, aff, 0)

    nh = GLA_HEADS
    dk = d // 2 // nh
    dv = d // nh
    main_w = 2 * nh * dk + 2 * nh * dv
    w_gla = gla_w_in[0]
    wl = jnp.pad(w_gla[:, main_w:], ((0, 0), (0, LANES - 2 * GLA_RANK))).astype(BF16)
    wgu = jnp.zeros((LANES, 2 * nh * dk), F32)
    wgu = wgu.at[:GLA_RANK, :nh * dk].set(gla_w_gate_up[0, 0])
    wgu = wgu.at[GLA_RANK:2 * GLA_RANK, nh * dk:].set(gla_w_gate_up[0, 1])
    bgu = jnp.concatenate([gla_b_gate[0, 0], gla_b_gate[0, 1]])[None, :]
    proj1, gates1 = _inproj_gla(x, mod[1], norm_mix[1:2], w_gla[:, :main_w].astype(BF16), wl, wgu, bgu,
                                tps=tps, tm=tm)
    s0t = state_gla[:, 0].swapaxes(-1, -2)
    o_f, o_b, _ = _gla(proj1, gates1, s0t, None, row0=0, nb=n_dec, t=ts, tb=min(512, ts), dk=dk, dv=dv)
    o_f, o_b, sfin = _gla(proj1, gates1, jnp.zeros((batch, 2, nh, dv, dk), F32), (o_f, o_b),
                          row0=n_dec_tok, nb=batch, t=seq, tb=min(512, seq), dk=dk, dv=dv)
    x, aff = _gla_post(o_f, o_b, proj1, x, mod[1], gla_o_norm[0][None, :], gla_w_out[0].astype(BF16),
                       norm_ffn[1:2], wr_pad[1], tps=tps, tm=tm, r_col=2 * nh * dk + nh * dv, ne=ne)
    x = moe_layer(x, aff, 1)

    gain = norm_out[None, :]
    y_sample = _final_norm(x, gain, row0=0, nrows=n_dec_tok, tm=tm).reshape(n_dec, ts, d)
    y_prompt = _final_norm(x, gain, row0=n_dec_tok, nrows=batch * seq, tm=tm).reshape(batch, seq, d)
    new_k = k_nrm[n_dec_tok:].reshape(batch, 1, seq, ATTN_KV_HEADS, HEAD_DIM)
    new_v = proj[n_dec_tok:, qw + kw:qw + 2 * kw].reshape(batch, 1, seq, ATTN_KV_HEADS, HEAD_DIM)
    new_s5 = hfin[..., :s5n].reshape(s5g, 2, 2, batch, s5n).transpose(3, 1, 0, 4, 2)[:, None]
    new_gla = sfin.swapaxes(-1, -2)[:, None]
    return (y_prompt, y_sample, new_k, new_v, new_s5, new_gla)
```

```python
import functools
import math

import jax
import jax.numpy as jnp
from jax import lax
from jax.experimental import pallas as pl
from jax.experimental.pallas import tpu as pltpu

F32 = jnp.float32
BF16 = jnp.bfloat16

RMS_EPS = 1e-6
GRID_W = 64
ROPE_THETA = 10000.0
ATTN_HEADS = 8
ATTN_KV_HEADS = 2
HEAD_DIM = 64
S5_GROUP_CH = 16
S5_CHUNK = 16
GLA_HEADS = 4
GLA_RANK = 16
GLA_GATE_NORM = 16.0
GLA_CHUNK = 64
EC_CAPACITY = 2
MOE_F_SUB = 256
MOE_F_PARTS = 2
LANES = 128
TOKEN_TILE = 512
VMEM_LIMIT = 56 * 1024 * 1024


def _cparams(*sem):
    return pltpu.CompilerParams(dimension_semantics=sem, vmem_limit_bytes=VMEM_LIMIT)


def _dot(a, b):
    return jnp.dot(a, b, preferred_element_type=F32)


def _dot_nt(a, b):
    return lax.dot_general(a, b, (((1,), (1,)), ((), ())), preferred_element_type=F32)


def _split(a):
    hi = a.astype(BF16)
    lo = (a - hi.astype(F32)).astype(BF16)
    return hi, lo


def _dot3(a, b):
    ah, al = _split(a)
    bh, bl = _split(b)
    return _dot(ah, bh) + _dot(al, bh) + _dot(ah, bl)


def _dot2(a, b_exact):
    ah, al = _split(a)
    return _dot(ah, b_exact) + _dot(al, b_exact)


def _sigmoid(x):
    return 1.0 / (1.0 + jnp.exp(-x))


def _silu(x):
    return x * _sigmoid(x)


def _gelu_tanh(x):
    c = math.sqrt(2.0 / math.pi)
    return 0.5 * x * (1.0 + jnp.tanh(c * (x + 0.044715 * (x * x * x))))


def _log_sigmoid(z):
    return jnp.minimum(z, 0.0) - jnp.log1p(jnp.exp(-jnp.abs(z)))


def _rms_mod(x, gain, scale, shift):
    ms = jnp.mean(x * x, axis=-1, keepdims=True)
    return (x * lax.rsqrt(ms + RMS_EPS)) * gain * (1.0 + scale) + shift


def _router_probs(xn, mod, gain, wr, ne):
    h = _rms_mod(xn, gain, mod[4:5], mod[3:4])
    logits = _dot3(h, wr)
    lane = lax.broadcasted_iota(jnp.int32, logits.shape, 1)
    logits = jnp.where(lane < ne, logits, -jnp.inf)
    m = jnp.max(logits, axis=-1, keepdims=True)
    p = jnp.exp(logits - m)
    return p / jnp.sum(p, axis=-1, keepdims=True)


def _mod_kernel(c_ref, w_ref, b_ref, o_ref):
    c = c_ref[...]
    o_ref[0] = _dot3(_silu(c), w_ref[0]) + b_ref[0]


def _modulation(cond, w_mod, b_mod):
    depth, d, n = w_mod.shape
    r = cond.shape[0]
    tn = n // 4
    return pl.pallas_call(
        _mod_kernel,
        name="modulation",
        grid=(depth, n // tn),
        in_specs=[
            pl.BlockSpec((r, d), lambda l, j: (0, 0)),
            pl.BlockSpec((1, d, tn), lambda l, j: (l, 0, j)),
            pl.BlockSpec((1, 1, tn), lambda l, j: (l, 0, j)),
        ],
        out_specs=pl.BlockSpec((1, r, tn), lambda l, j: (l, 0, j)),
        out_shape=jax.ShapeDtypeStruct((depth, r, n), F32),
        compiler_params=_cparams("parallel", "parallel"),
    )(cond, w_mod, b_mod.reshape(depth, 1, n))


def _two_group_specs(tm, d, nd_tiles):
    return [pl.BlockSpec((tm, d), lambda i: (jnp.minimum(i, nd_tiles - 1), 0)),
            pl.BlockSpec((tm, d), lambda i: (jnp.maximum(i - nd_tiles, 0), 0))]


def _two_group_tile(xd_ref, xc_ref, nd_tiles):
    return jnp.where(pl.program_id(0) < nd_tiles, xd_ref[...], xc_ref[...])


def _inproj_ab_kernel(xd_ref, xc_ref, mod_ref, g_ref, w_ref, qkv_ref, u_ref, *, nd_tiles):
    m = mod_ref[0]
    x = _two_group_tile(xd_ref, xc_ref, nd_tiles)
    h = _rms_mod(x, g_ref[...], m[1:2], m[0:1]).astype(BF16)
    nq = qkv_ref.shape[1]
    qkv_ref[...] = _dot(h, w_ref[:, :nq])
    u_ref[...] = _dot(h, w_ref[:, nq:])


def _inproj_ab(x_dec, x_ctx, mod, gain, w, *, n_qkv, tps, tm):
    d = x_dec.shape[1]
    nd_tiles = x_dec.shape[0] // tm
    ntok = x_dec.shape[0] + x_ctx.shape[0]
    n = w.shape[1]
    return pl.pallas_call(
        functools.partial(_inproj_ab_kernel, nd_tiles=nd_tiles),
        name="inproj_ab",
        grid=(ntok // tm,),
        in_specs=_two_group_specs(tm, d, nd_tiles) + [
            pl.BlockSpec((1, 8, d), lambda i: (i // tps, 0, 0)),
            pl.BlockSpec((1, d), lambda i: (0, 0)),
            pl.BlockSpec((d, n), lambda i: (0, 0)),
        ],
        out_specs=[pl.BlockSpec((tm, n_qkv), lambda i: (i, 0)),
                   pl.BlockSpec((tm, n - n_qkv), lambda i: (i, 0))],
        out_shape=[jax.ShapeDtypeStruct((ntok, n_qkv), F32),
                   jax.ShapeDtypeStruct((ntok, n - n_qkv), F32)],
        compiler_params=_cparams("parallel"),
    )(x_dec, x_ctx, mod, gain, w)


def _inproj_gla_kernel(x_ref, mod_ref, g_ref, w_ref, wl_ref, wg_ref, bg_ref, qk_ref, v_ref, r_ref, gate_ref):
    m = mod_ref[0]
    h = _rms_mod(x_ref[...], g_ref[...], m[1:2], m[0:1]).astype(BF16)
    c0 = 0
    for ref in (qk_ref, v_ref, r_ref):
        w = ref.shape[1]
        for s in range(0, w, 512):
            ref[:, s:s + 512] = _dot(h, w_ref[:, c0 + s:c0 + s + 512]).astype(ref.dtype)
        c0 += w
    low = _dot(h, wl_ref[...])
    z = _dot3(low, wg_ref[...]) + bg_ref[...]
    gate_ref[...] = _log_sigmoid(z) / GLA_GATE_NORM


def _inproj_gla(x, mod, gain, w, wl, wg, bg, *, widths, tps, tm):
    ntok, d = x.shape
    n = w.shape[1]
    ng = wg.shape[1]
    wqk, wv, wr = widths
    row = lambda width: pl.BlockSpec((tm, width), lambda i: (i, 0))
    return pl.pallas_call(
        _inproj_gla_kernel,
        name="inproj_gla",
        grid=(ntok // tm,),
        in_specs=[
            pl.BlockSpec((tm, d), lambda i: (i, 0)),
            pl.BlockSpec((1, 8, d), lambda i: (i // tps, 0, 0)),
            pl.BlockSpec((1, d), lambda i: (0, 0)),
            pl.BlockSpec((d, n), lambda i: (0, 0)),
            pl.BlockSpec((d, LANES), lambda i: (0, 0)),
            pl.BlockSpec((LANES, ng), lambda i: (0, 0)),
            pl.BlockSpec((1, ng), lambda i: (0, 0)),
        ],
        out_specs=[row(wqk), row(wv), row(wr), row(ng)],
        out_shape=[jax.ShapeDtypeStruct((ntok, wqk), F32),
                   jax.ShapeDtypeStruct((ntok, wv), BF16),
                   jax.ShapeDtypeStruct((ntok, wr), F32),
                   jax.ShapeDtypeStruct((ntok, ng), F32)],
        compiler_params=_cparams("parallel"),
    )(x, mod, gain, w, wl, wg, bg)


def _qkprep_kernel(q_ref, k_ref, v_ref, cos_ref, sin_ref, qg_ref, kg_ref, oq_ref, ok_ref,
                   qo_ref, ko_ref, kn_ref, vo_ref):
    cos = cos_ref[0]
    sin = sin_ref[0]

    def prep(x, gain, ones):
        width = x.shape[1]
        ssq = _dot2(x * x, ones)
        xn = x * lax.rsqrt(ssq * (1.0 / HEAD_DIM) + RMS_EPS) * gain
        lane = lax.broadcasted_iota(jnp.int32, x.shape, 1)
        first = (lane & 16) == 0
        partner = jnp.where(first, pltpu.roll(xn, width - 16, 1), pltpu.roll(xn, 16, 1))
        rep = width // LANES
        cosw = jnp.concatenate([cos] * rep, axis=1) if rep > 1 else cos
        sinw = jnp.concatenate([sin] * rep, axis=1) if rep > 1 else sin
        return xn, xn * cosw + partner * sinw

    _, qr = prep(q_ref[...], qg_ref[...], oq_ref[...])
    qo_ref[...] = (qr * (HEAD_DIM ** -0.5)).astype(BF16)
    kn, kr = prep(k_ref[...], kg_ref[...], ok_ref[...])
    kn_ref[...] = kn
    ko_ref[...] = kr.astype(BF16)
    vo_ref[...] = v_ref[...].astype(BF16)


def _qkprep(proj, cos_tab, sin_tab, q_gain, k_gain, *, ts, tm, n_rope_seq):
    ntok = proj.shape[0]
    qw = ATTN_HEADS * HEAD_DIM
    kw = ATTN_KV_HEADS * HEAD_DIM
    tps = ts // tm

    def head_ones(width):
        a = jnp.arange(width) // HEAD_DIM
        return (a[:, None] == a[None, :]).astype(BF16)

    def tab_map(i):
        return (jnp.minimum(i // tps, n_rope_seq) // n_rope_seq, i % tps, 0)

    return pl.pallas_call(
        _qkprep_kernel,
        name="qkprep",
        grid=(ntok // tm,),
        in_specs=[
            pl.BlockSpec((tm, qw), lambda i: (i, 0)),
            pl.BlockSpec((tm, kw), lambda i: (i, qw // kw)),
            pl.BlockSpec((tm, kw), lambda i: (i, qw // kw + 1)),
            pl.BlockSpec((1, tm, LANES), tab_map),
            pl.BlockSpec((1, tm, LANES), tab_map),
            pl.BlockSpec((1, qw), lambda i: (0, 0)),
            pl.BlockSpec((1, kw), lambda i: (0, 0)),
            pl.BlockSpec((qw, qw), lambda i: (0, 0)),
            pl.BlockSpec((kw, kw), lambda i: (0, 0)),
        ],
        out_specs=[
            pl.BlockSpec((tm, qw), lambda i: (i, 0)),
            pl.BlockSpec((tm, kw), lambda i: (i, 0)),
            pl.BlockSpec((tm, kw), lambda i: (i, 0)),
            pl.BlockSpec((tm, kw), lambda i: (i, 0)),
        ],
        out_shape=[
            jax.ShapeDtypeStruct((ntok, qw), BF16),
            jax.ShapeDtypeStruct((ntok, kw), BF16),
            jax.ShapeDtypeStruct((ntok, kw), F32),
            jax.ShapeDtypeStruct((ntok, kw), BF16),
        ],
        compiler_params=_cparams("parallel"),
    )(proj, proj, proj, cos_tab, sin_tab,
      jnp.tile(q_gain, ATTN_HEADS).reshape(1, qw), jnp.tile(k_gain, ATTN_KV_HEADS).reshape(1, kw),
      head_ones(qw), head_ones(kw))


def _rope_tables(t):
    nfreq = HEAD_DIM // 4
    inv = ROPE_THETA ** (-jnp.arange(nfreq, dtype=F32) / nfreq)
    pos = jnp.arange(t)
    row = (pos // GRID_W).astype(F32)[:, None] * inv[None, :]
    col = (pos % GRID_W).astype(F32)[:, None] * inv[None, :]
    cos = jnp.concatenate([jnp.cos(row), jnp.cos(row), jnp.cos(col), jnp.cos(col)], axis=1)
    sin = jnp.concatenate([-jnp.sin(row), jnp.sin(row), -jnp.sin(col), jnp.sin(col)], axis=1)
    rep = LANES // HEAD_DIM
    cos = jnp.tile(cos, (1, rep))
    sin = jnp.tile(sin, (1, rep))
    return (jnp.stack([cos, jnp.ones_like(cos)]), jnp.stack([sin, jnp.zeros_like(sin)]))


def _attn_kernel(q_ref, kt_ref, v_ref, *rest):
    o_ref = rest[-1]
    q = q_ref[...]
    v = v_ref[0]
    group = ATTN_HEADS // ATTN_KV_HEADS
    outs = []
    for h in range(ATTN_HEADS):
        kv = h // group
        qh = q[:, h * HEAD_DIM:(h + 1) * HEAD_DIM]
        kt = kt_ref[0, kv * HEAD_DIM:(kv + 1) * HEAD_DIM, :]
        s = _dot(qh, kt)
        p = jnp.exp(s - jnp.max(s, axis=-1, keepdims=True))
        l = jnp.sum(p, axis=-1, keepdims=True)
        o = _dot(p.astype(BF16), v)
        outs.append(o[:, kv * HEAD_DIM:(kv + 1) * HEAD_DIM] / l)
    o_ref[...] = jnp.concatenate(outs, axis=1).astype(BF16)


def _attention(q, keys_t, vals, prev, *, row0, t, tq):
    b, kw, s = keys_t.shape
    ntok, qw = q.shape
    nq = t // tq
    rb0 = row0 // tq
    extra = [] if prev is None else [prev]
    return pl.pallas_call(
        _attn_kernel,
        name="attention",
        grid=(b, nq),
        in_specs=[
            pl.BlockSpec((tq, qw), lambda bi, i: (rb0 + bi * nq + i, 0)),
            pl.BlockSpec((1, kw, s), lambda bi, i: (bi, 0, 0)),
            pl.BlockSpec((1, s, kw), lambda bi, i: (bi, 0, 0)),
        ] + [pl.BlockSpec(memory_space=pl.ANY)] * len(extra),
        out_specs=pl.BlockSpec((tq, qw), lambda bi, i: (rb0 + bi * nq + i, 0)),
        out_shape=jax.ShapeDtypeStruct((ntok, qw), BF16),
        input_output_aliases={3: 0} if extra else {},
        compiler_params=_cparams("parallel", "parallel"),
    )(q, keys_t, vals, *extra)


def _s5_kernel(x_ref, e_ref, m_ref, f_ref, lam_ref, h0_ref, y_ref, hfin_ref, s_ref, hin_ref, *, nb, nc):
    x = x_ref[0]
    s_ref[...] = _dot3(x, e_ref[0])
    lam = lam_ref[0]
    lfr = jnp.broadcast_to(lam[0:1], (nb, LANES))
    lfi = jnp.broadcast_to(lam[1:2], (nb, LANES))
    lbr = jnp.broadcast_to(lam[2:3], (nb, LANES))
    lbi = jnp.broadcast_to(lam[3:4], (nb, LANES))

    def step(c, carry):
        hfr, hfi, hbr, hbi = carry
        rf = pl.multiple_of(c * nb, nb)
        rb = pl.multiple_of((nc - 1 - c) * nb, nb)
        hin_ref[pl.ds(rf, nb), 0:LANES] = hfr
        hin_ref[pl.ds(rf, nb), LANES:2 * LANES] = hfi
        hin_ref[pl.ds(rb, nb), 2 * LANES:3 * LANES] = hbr
        hin_ref[pl.ds(rb, nb), 3 * LANES:4 * LANES] = hbi
        sfr = s_ref[pl.ds(rf, nb), 0:LANES]
        sfi = s_ref[pl.ds(rf, nb), LANES:2 * LANES]
        sbr = s_ref[pl.ds(rb, nb), 2 * LANES:3 * LANES]
        sbi = s_ref[pl.ds(rb, nb), 3 * LANES:4 * LANES]
        return (lfr * hfr - lfi * hfi + sfr, lfr * hfi + lfi * hfr + sfi,
                lbr * hbr - lbi * hbi + sbr, lbr * hbi + lbi * hbr + sbi)

    h0 = h0_ref[0]
    hfr, hfi, hbr, hbi = lax.fori_loop(0, nc, step, (h0[0], h0[1], h0[2], h0[3]))
    hfin_ref[0, 0] = hfr
    hfin_ref[0, 1] = hfi
    hfin_ref[0, 2] = hbr
    hfin_ref[0, 3] = hbi
    y_ref[0] = _dot3(x, m_ref[0]) + _dot3(hin_ref[...], f_ref[0])


def _s5(xg, e_mat, m_mat, f_mat, lam_l, h0, *, nb, nc):
    g, r, w = xg.shape
    sw = 4 * LANES
    return pl.pallas_call(
        functools.partial(_s5_kernel, nb=nb, nc=nc),
        name="s5",
        grid=(g,),
        in_specs=[
            pl.BlockSpec((1, r, w), lambda i: (i, 0, 0)),
            pl.BlockSpec((1, w, sw), lambda i: (i, 0, 0)),
            pl.BlockSpec((1, w, w), lambda i: (i, 0, 0)),
            pl.BlockSpec((1, sw, w), lambda i: (i, 0, 0)),
            pl.BlockSpec((1, 8, LANES), lambda i: (i, 0, 0)),
            pl.BlockSpec((1, 4, nb, LANES), lambda i: (i, 0, 0, 0)),
        ],
        out_specs=[pl.BlockSpec((1, r, w), lambda i: (i, 0, 0)),
                   pl.BlockSpec((1, 4, nb, LANES), lambda i: (i, 0, 0, 0))],
        out_shape=[jax.ShapeDtypeStruct((g, r, w), F32),
                   jax.ShapeDtypeStruct((g, 4, nb, LANES), F32)],
        scratch_shapes=[pltpu.VMEM((r, sw), F32), pltpu.VMEM((r, sw), F32)],
        compiler_params=_cparams("parallel"),
    )(xg, e_mat, m_mat, f_mat, lam_l, h0)


def _s5_matrices(lam_re, lam_im, log_dt, b_re, b_im, c_re, c_im, d_skip):
    hi = lax.Precision.HIGHEST
    L = S5_CHUNK
    _, g, n = lam_re.shape
    p = b_re.shape[-1]
    dt = jnp.exp(log_dt)[..., None]
    mag = jnp.exp(lam_re * dt)
    lbr, lbi = mag * jnp.cos(lam_im * dt), mag * jnp.sin(lam_im * dt)
    den = lam_re * lam_re + lam_im * lam_im
    cr = ((lbr - 1.0) * lam_re + lbi * lam_im) / den
    ci = (lbi * lam_re - (lbr - 1.0) * lam_im) / den
    bbr = cr[..., None] * b_re - ci[..., None] * b_im
    bbi = cr[..., None] * b_im + ci[..., None] * b_re
    pr, pi = [jnp.ones_like(lbr)], [jnp.zeros_like(lbr)]
    for _ in range(L):
        pr, pi = pr + [pr[-1] * lbr - pi[-1] * lbi], pi + [pr[-1] * lbi + pi[-1] * lbr]
    pwr, pwi = jnp.stack(pr, axis=2), jnp.stack(pi, axis=2)
    cpr = c_re[:, :, None] * pwr[:, :, :, None, :] - c_im[:, :, None] * pwi[:, :, :, None, :]
    cpi = c_re[:, :, None] * pwi[:, :, :, None, :] + c_im[:, :, None] * pwr[:, :, :, None, :]
    kern = (jnp.einsum('dgkpn,dgnq->dgkpq', cpr[:, :, :L], bbr, precision=hi)
            - jnp.einsum('dgkpn,dgnq->dgkpq', cpi[:, :, :L], bbi, precision=hi))
    step = jnp.arange(L)
    lag_f = step[None, :] - step[:, None]
    lag_b = -lag_f

    def toeplitz(kd, lag):
        t = jnp.take(kd, jnp.clip(lag, 0, L - 1), axis=1)
        t = jnp.where((lag >= 0)[None, :, :, None, None], t, 0.0)
        return t.transpose(0, 1, 4, 2, 3).reshape(g, L * p, L * p)

    eye_l = jnp.eye(L, dtype=F32)
    dmat = jnp.einsum('ji,gq,qp->gjqip', eye_l, d_skip, jnp.eye(p, dtype=F32)).reshape(g, L * p, L * p)
    m_mat = toeplitz(kern[0], lag_f) + toeplitz(kern[1], lag_b) + dmat

    def e_block(pw_r, pw_i, b_r, b_i):
        er = pw_r[:, :, None, :] * b_r.transpose(0, 2, 1)[:, None] - pw_i[:, :, None, :] * b_i.transpose(0, 2, 1)[:, None]
        ei = pw_r[:, :, None, :] * b_i.transpose(0, 2, 1)[:, None] + pw_i[:, :, None, :] * b_r.transpose(0, 2, 1)[:, None]
        return er.reshape(g, L * p, n), ei.reshape(g, L * p, n)

    def pad_cols(a):
        return jnp.pad(a, ((0, 0), (0, 0), (0, LANES - n)))

    efr, efi = e_block(pwr[0][:, L - 1 - step], pwi[0][:, L - 1 - step], bbr[0], bbi[0])
    ebr, ebi = e_block(pwr[1][:, step], pwi[1][:, step], bbr[1], bbi[1])
    e_mat = jnp.concatenate([pad_cols(efr), pad_cols(efi), pad_cols(ebr), pad_cols(ebi)], axis=2)

    def f_block(c_r, c_i):
        fr = c_r.transpose(0, 3, 1, 2).reshape(g, n, L * p)
        fi = -c_i.transpose(0, 3, 1, 2).reshape(g, n, L * p)
        return fr, fi

    def pad_rows(a):
        return jnp.pad(a, ((0, 0), (0, LANES - n), (0, 0)))

    ffr, ffi = f_block(cpr[0][:, step + 1], cpi[0][:, step + 1])
    fbr, fbi = f_block(cpr[1][:, L - step], cpi[1][:, L - step])
    f_mat = jnp.concatenate([pad_rows(ffr), pad_rows(ffi), pad_rows(fbr), pad_rows(fbi)], axis=1)

    lam_l = jnp.stack([pwr[0, :, L], pwi[0, :, L], pwr[1, :, L], pwi[1, :, L]], axis=1)
    lam_l = jnp.pad(lam_l, ((0, 0), (0, 4), (0, LANES - n)))
    return e_mat, m_mat, f_mat, lam_l


def _s5_to_chunks(u, nb, t, g):
    p = u.shape[1] // g
    nc = t // S5_CHUNK
    x = u.reshape(nb, nc, S5_CHUNK, g, p).transpose(3, 1, 0, 2, 4)
    return x.reshape(g, nc * nb, S5_CHUNK * p)


def _s5_from_chunks(y, nb, t):
    g, _, w = y.shape
    p = w // S5_CHUNK
    nc = t // S5_CHUNK
    return y.reshape(g, nc, nb, S5_CHUNK, p).transpose(2, 1, 3, 0, 4).reshape(nb * t, g * p)


def _ab_post_kernel(y_ref, oa_ref, xd_ref, xc_ref, mod_ref, wglu_ref, bglu_ref, woa_ref, wos_ref, gffn_ref,
                    wr_ref, xo_ref, aff_ref, *, ne, nd_tiles):
    m = mod_ref[0]
    half = wos_ref.shape[0]
    glu = _dot(_gelu_tanh(y_ref[...]).astype(BF16), wglu_ref[...]) + bglu_ref[...]
    o_s5 = glu[:, :half] * _sigmoid(glu[:, half:])
    out = _dot(oa_ref[...], woa_ref[...]) + _dot(o_s5.astype(BF16), wos_ref[...])
    xn = _two_group_tile(xd_ref, xc_ref, nd_tiles) + m[2:3] * out
    xo_ref[...] = xn
    aff_ref[...] = _router_probs(xn, m, gffn_ref[...], wr_ref[...], ne)


def _ab_post(y, o_attn, x_dec, x_ctx, mod, w_glu, b_glu, w_out_a, w_out_s, g_ffn, w_router, *, tps, tm, ne):
    d = x_dec.shape[1]
    nd_tiles = x_dec.shape[0] // tm
    ntok = x_dec.shape[0] + x_ctx.shape[0]
    hw = y.shape[1]
    full = lambda shape: pl.BlockSpec(shape, lambda i: (0,) * len(shape))
    return pl.pallas_call(
        functools.partial(_ab_post_kernel, ne=ne, nd_tiles=nd_tiles),
        name="ab_post",
        grid=(ntok // tm,),
        in_specs=[
            pl.BlockSpec((tm, hw), lambda i: (i, 0)),
            pl.BlockSpec((tm, o_attn.shape[1]), lambda i: (i, 0)),
        ] + _two_group_specs(tm, d, nd_tiles) + [
            pl.BlockSpec((1, 8, d), lambda i: (i // tps, 0, 0)),
            full(w_glu.shape), full(b_glu.shape), full(w_out_a.shape), full(w_out_s.shape),
            full(g_ffn.shape), full(w_router.shape),
        ],
        out_specs=[pl.BlockSpec((tm, d), lambda i: (i, 0)),
                   pl.BlockSpec((tm, LANES), lambda i: (i, 0))],
        out_shape=[jax.ShapeDtypeStruct((ntok, d), F32),
                   jax.ShapeDtypeStruct((ntok, LANES), F32)],
        compiler_params=_cparams("parallel"),
    )(y, o_attn, x_dec, x_ctx, mod, w_glu, b_glu, w_out_a, w_out_s, g_ffn, w_router)


def _gla_post_kernel(of_ref, ob_ref, r_ref, x_ref, mod_ref, gn_ref, wo_ref, gffn_ref, wr_ref, xo_ref, aff_ref,
                     *, ne):
    m = mod_ref[0]
    o = of_ref[...] + ob_ref[...]
    dv = gn_ref.shape[1]
    parts = []
    for h in range(GLA_HEADS):
        oh = o[:, h * dv:(h + 1) * dv]
        ms = jnp.mean(oh * oh, axis=-1, keepdims=True)
        parts.append(oh * lax.rsqrt(ms + RMS_EPS) * gn_ref[...])
    on = jnp.concatenate(parts, axis=1) * _silu(r_ref[...])
    out = _dot(on.astype(BF16), wo_ref[...])
    xn = x_ref[...] + m[2:3] * out
    xo_ref[...] = xn
    aff_ref[...] = _router_probs(xn, m, gffn_ref[...], wr_ref[...], ne)


def _gla_post(o_fwd, o_bwd, r, x, mod, g_norm, w_out, g_ffn, w_router, *, tps, tm, ne):
    ntok, d = x.shape
    vw = o_fwd.shape[1]
    full = lambda shape: pl.BlockSpec(shape, lambda i: (0,) * len(shape))
    return pl.pallas_call(
        functools.partial(_gla_post_kernel, ne=ne),
        name="gla_post",
        grid=(ntok // tm,),
        in_specs=[
            pl.BlockSpec((tm, vw), lambda i: (i, 0)),
            pl.BlockSpec((tm, vw), lambda i: (i, 0)),
            pl.BlockSpec((tm, vw), lambda i: (i, 0)),
            pl.BlockSpec((tm, d), lambda i: (i, 0)),
            pl.BlockSpec((1, 8, d), lambda i: (i // tps, 0, 0)),
            full(g_norm.shape), full(w_out.shape), full(g_ffn.shape), full(w_router.shape),
        ],
        out_specs=[pl.BlockSpec((tm, d), lambda i: (i, 0)),
                   pl.BlockSpec((tm, LANES), lambda i: (i, 0))],
        out_shape=[jax.ShapeDtypeStruct((ntok, d), F32),
                   jax.ShapeDtypeStruct((ntok, LANES), F32)],
        compiler_params=_cparams("parallel"),
    )(o_fwd, o_bwd, r, x, mod, g_norm, w_out, g_ffn, w_router)


def _gla_kernel(qf_ref, kf_ref, vf_ref, gf_ref, qb_ref, kb_ref, vb_ref, gb_ref, s0_ref, *rest,
                nchunk, dk, dv, nh):
    of_ref, ob_ref, sfin_ref, st_ref, qd_ref, ke_ref, tot_ref = rest[-7:]
    j = pl.program_id(1)
    c = GLA_CHUNK
    tb = nchunk * c

    @pl.when(j == 0)
    def _():
        st_ref[...] = s0_ref[0]

    row = lax.broadcasted_iota(jnp.int32, (tb, tb), 0)
    col = lax.broadcasted_iota(jnp.int32, (tb, tb), 1)
    same = (row // c) == (col // c)
    same_b = same.astype(BF16)
    dirs = ((qf_ref, kf_ref, vf_ref, gf_ref, of_ref, same & (row >= col)),
            (qb_ref, kb_ref, vb_ref, gb_ref, ob_ref, same & (row <= col)))
    for d, (q_ref, k_ref, v_ref, g_ref, o_ref, tri) in enumerate(dirs):
        gh, gl = _split(g_ref[...])
        tri_b = tri.astype(BF16)
        cum = _dot(tri_b, gh) + _dot(tri_b, gl)
        tot = _dot(same_b, gh) + _dot(same_b, gl)
        k = k_ref[...]
        q_dec = (q_ref[...] * (dk ** -0.5) * jnp.exp(cum)).astype(BF16)
        k_inv = (k * jnp.exp(-cum)).astype(BF16)
        qd_ref[d] = q_dec
        ke_ref[d] = (k * jnp.exp(tot - cum)).astype(BF16)
        tot_ref[d] = tot
        for h in range(nh):
            ks = slice(h * dk, (h + 1) * dk)
            vs = slice(h * dv, (h + 1) * dv)
            scores = jnp.where(tri, _dot_nt(q_dec[:, ks], k_inv[:, ks]), 0.0)
            o_ref[:, vs] = _dot(scores.astype(BF16), v_ref[:, vs])

    def step(i, carry):
        for d, (_, _, v_ref, _, o_ref, _) in enumerate(dirs):
            r0 = pl.multiple_of((i if d == 0 else nchunk - 1 - i) * c, c)
            decay = jnp.exp(tot_ref[d, pl.ds(r0, 1), :])
            for h in range(nh):
                ks = slice(h * dk, (h + 1) * dk)
                vs = slice(h * dv, (h + 1) * dv)
                st = st_ref[d, h]
                o_ref[pl.ds(r0, c), vs] += _dot_nt(qd_ref[d, pl.ds(r0, c), ks], st.astype(BF16))
                vt = v_ref[pl.ds(r0, c), vs].astype(F32).T.astype(BF16)
                st_ref[d, h] = st * decay[:, ks] + _dot(vt, ke_ref[d, pl.ds(r0, c), ks])
        return carry

    lax.fori_loop(0, nchunk, step, 0)

    @pl.when(j == pl.num_programs(1) - 1)
    def _():
        sfin_ref[0] = st_ref[...]


def _dot_tri(tri_b, g):
    gh, gl = _split(g)
    return _dot(tri_b, gh) + _dot(tri_b, gl)


def _gla(qk, v, gates, s0t, prev, *, row0, nb, t, tb, dk, dv):
    h = GLA_HEADS
    ntok = qk.shape[0]
    nblk = t // tb
    rb0 = row0 // tb
    kw, vw = h * dk, h * dv
    fwd = lambda b, j: rb0 + b * nblk + j
    bwd = lambda b, j: rb0 + b * nblk + nblk - 1 - j

    def views(rows, gate_col):
        return [pl.BlockSpec((tb, kw), lambda b, j: (rows(b, j), 0)),
                pl.BlockSpec((tb, kw), lambda b, j: (rows(b, j), 1)),
                pl.BlockSpec((tb, vw), lambda b, j: (rows(b, j), 0)),
                pl.BlockSpec((tb, kw), lambda b, j: (rows(b, j), gate_col))]

    extra = [] if prev is None else list(prev)
    state_spec = pl.BlockSpec((1, 2, h, dv, dk), lambda b, j: (b, 0, 0, 0, 0))
    return pl.pallas_call(
        functools.partial(_gla_kernel, nchunk=tb // GLA_CHUNK, dk=dk, dv=dv, nh=h),
        name="gla",
        input_output_aliases={9: 0, 10: 1} if extra else {},
        grid=(nb, nblk),
        in_specs=views(fwd, 0) + views(bwd, 1) + [state_spec] + [pl.BlockSpec(memory_space=pl.ANY)] * len(extra),
        out_specs=[
            pl.BlockSpec((tb, vw), lambda b, j: (fwd(b, j), 0)),
            pl.BlockSpec((tb, vw), lambda b, j: (bwd(b, j), 0)),
            state_spec,
        ],
        out_shape=[jax.ShapeDtypeStruct((ntok, vw), F32),
                   jax.ShapeDtypeStruct((ntok, vw), F32),
                   jax.ShapeDtypeStruct((nb, 2, h, dv, dk), F32)],
        scratch_shapes=[pltpu.VMEM((2, h, dv, dk), F32), pltpu.VMEM((2, tb, kw), BF16),
                        pltpu.VMEM((2, tb, kw), BF16), pltpu.VMEM((2, tb, kw), F32)],
        compiler_params=_cparams("parallel", "arbitrary"),
    )(qk, qk, v, gates, qk, qk, v, gates, s0t, *extra)


def _moe_kernel(idx_ref, gsm_ref, x_ref, mod_ref, gffn_ref, wg_ref, wu_ref, wd_ref, o_ref,
                xs_ref, gbuf_ref, ya_ref, yb_ref, *, cap, ne, nf):
    e = pl.program_id(1)
    f = pl.program_id(2)
    m = mod_ref[0]
    g2 = m[5:6]
    nsub = wg_ref.shape[1]
    rows_per = cap // (nf * nsub)
    e_next = jnp.minimum(e + 1, ne - 1)
    e_prev = jnp.maximum(e - 1, 0)

    def gather_row(en, r):
        gbuf_ref[pl.ds(r, 1), :] = x_ref[0, pl.ds(idx_ref[0, en, 0, r], 1), :]

    def scatter_row(ep, y_ref, r):
        i = idx_ref[0, ep, 0, r]
        o_ref[0, pl.ds(i, 1), :] += y_ref[pl.ds(r, 1), :] * gsm_ref[0, ep, 0, r] * g2

    @pl.when((e == 0) & (f == 0))
    def _():
        o_ref[...] = x_ref[...]
        yb_ref[...] = jnp.zeros(yb_ref.shape, F32)

        def body(r, carry):
            gather_row(0, r)
            return carry
        lax.fori_loop(0, cap, body, 0, unroll=8)

    @pl.when(f == 0)
    def _():
        xs_ref[...] = _rms_mod(gbuf_ref[...], gffn_ref[...], m[4:5], m[3:4]).astype(BF16)

    def expert_part(y_cur, y_prev):
        @pl.when(f == 0)
        def _():
            y_cur[...] = jnp.zeros(y_cur.shape, F32)

        def sub_chunk(k, carry):
            r0 = pl.multiple_of((f * nsub + k) * rows_per, rows_per)
            for r in range(rows_per):
                gather_row(e_next, r0 + r)
                scatter_row(e_prev, y_prev, r0 + r)
            xs = xs_ref[...]
            hid = _silu(_dot(xs, wg_ref[0, k])) * _dot(xs, wu_ref[0, k])
            y_cur[...] += _dot(hid.astype(BF16), wd_ref[0, k])
            return carry

        lax.fori_loop(0, nsub, sub_chunk, 0)

        @pl.when((e == ne - 1) & (f == nf - 1))
        def _():
            def body(r, carry):
                scatter_row(e, y_cur, r)
                return carry
            lax.fori_loop(0, cap, body, 0, unroll=8)

    @pl.when(e % 2 == 0)
    def _():
        expert_part(ya_ref, yb_ref)

    @pl.when(e % 2 == 1)
    def _():
        expert_part(yb_ref, ya_ref)


def _cast_kernel(w_ref, o_ref):
    o_ref[...] = w_ref[...].astype(BF16).reshape(o_ref.shape)


def _split_cast_cols(w, sub):
    nl, ne, d, ff = w.shape
    return pl.pallas_call(
        _cast_kernel,
        name="expert_weight_cols",
        grid=(nl, ne, ff // sub),
        in_specs=[pl.BlockSpec((1, 1, d, sub), lambda l, e, k: (l, e, 0, k))],
        out_specs=pl.BlockSpec((1, 1, 1, d, sub), lambda l, e, k: (l, e, k, 0, 0)),
        out_shape=jax.ShapeDtypeStruct((nl, ne, ff // sub, d, sub), BF16),
        compiler_params=_cparams("parallel", "parallel", "parallel"),
    )(w)


def _split_cast_rows(w, sub):
    nl, ne, ff, d = w.shape
    rows = 4 * sub
    return pl.pallas_call(
        _cast_kernel,
        name="expert_weight_rows",
        grid=(nl, ne, ff // rows),
        in_specs=[pl.BlockSpec((1, 1, rows, d), lambda l, e, k: (l, e, k, 0))],
        out_specs=pl.BlockSpec((1, 1, rows // sub, sub, d), lambda l, e, k: (l, e, k, 0, 0)),
        out_shape=jax.ShapeDtypeStruct((nl, ne, ff // sub, sub, d), BF16),
        compiler_params=_cparams("parallel", "parallel", "parallel"),
    )(w)


def _moe(x3, idx, gates, mod, g_ffn, w_gate, w_up, w_down, *, layer, nf):
    ns, ts, d = x3.shape
    _, ne, nsub_all, _, sub = w_gate.shape
    cap = idx.shape[2]
    nsub = nsub_all // nf
    return pl.pallas_call(
        functools.partial(_moe_kernel, cap=cap, ne=ne, nf=nf),
        name="moe",
        grid=(ns, ne, nf),
        in_specs=[
            pl.BlockSpec((1, ne, 1, cap), lambda s, e, f: (s, 0, 0, 0), memory_space=pltpu.SMEM),
            pl.BlockSpec((1, ne, 1, cap), lambda s, e, f: (s, 0, 0, 0), memory_space=pltpu.SMEM),
            pl.BlockSpec((1, ts, d), lambda s, e, f: (s, 0, 0), pipeline_mode=pl.Buffered(1)),
            pl.BlockSpec((1, 8, d), lambda s, e, f: (s, 0, 0)),
            pl.BlockSpec((1, d), lambda s, e, f: (0, 0)),
            pl.BlockSpec((None, 1, nsub, d, sub), lambda s, e, f: (layer, e, f, 0, 0)),
            pl.BlockSpec((None, 1, nsub, d, sub), lambda s, e, f: (layer, e, f, 0, 0)),
            pl.BlockSpec((None, 1, nsub, sub, d), lambda s, e, f: (layer, e, f, 0, 0)),
        ],
        out_specs=pl.BlockSpec((1, ts, d), lambda s, e, f: (s, 0, 0), pipeline_mode=pl.Buffered(1)),
        out_shape=jax.ShapeDtypeStruct((ns, ts, d), F32),
        scratch_shapes=[pltpu.VMEM((cap, d), BF16), pltpu.VMEM((cap, d), F32),
                        pltpu.VMEM((cap, d), F32), pltpu.VMEM((cap, d), F32)],
        compiler_params=_cparams("parallel", "arbitrary", "arbitrary"),
    )(idx.reshape(ns, ne, 1, cap), gates.reshape(ns, ne, 1, cap), x3, mod, g_ffn, w_gate, w_up, w_down)


def _route(aff, *, n_dec, ts, batch, seq, ne):
    cap_dec = EC_CAPACITY * ts // ne
    cap_ctx = EC_CAPACITY * seq // ne
    a = aff[:, :ne]
    a_dec = a[:n_dec * ts].reshape(n_dec, ts, ne).swapaxes(1, 2)
    g_dec, i_dec = lax.top_k(a_dec, cap_dec)
    a_ctx = a[n_dec * ts:].reshape(batch, seq, ne).swapaxes(1, 2)
    g_ctx, i_ctx = lax.top_k(a_ctx, cap_ctx)
    i_ctx = i_ctx + (jnp.arange(batch) * seq)[:, None, None]
    g_ctx = g_ctx.swapaxes(0, 1).reshape(1, ne, batch * cap_ctx)
    i_ctx = i_ctx.swapaxes(0, 1).reshape(1, ne, batch * cap_ctx)
    return (jnp.concatenate([i_dec, i_ctx], axis=0).astype(jnp.int32),
            jnp.concatenate([g_dec, g_ctx], axis=0))


def _final_kernel(x_ref, g_ref, o_ref):
    x = x_ref[...]
    ms = jnp.mean(x * x, axis=-1, keepdims=True)
    o_ref[...] = (x * lax.rsqrt(ms + RMS_EPS)) * g_ref[...]


def _final_norm(x, gain, *, row0, nrows, tm):
    d = x.shape[1]
    rb0 = row0 // tm
    return pl.pallas_call(
        _final_kernel,
        name="final_norm",
        grid=(nrows // tm,),
        in_specs=[pl.BlockSpec((tm, d), lambda i: (rb0 + i, 0)),
                  pl.BlockSpec((1, d), lambda i: (0, 0))],
        out_specs=pl.BlockSpec((tm, d), lambda i: (i, 0)),
        out_shape=jax.ShapeDtypeStruct((nrows, d), F32),
        compiler_params=_cparams("parallel"),
    )(x, gain)


def kernel(x_prompt, x_sample, c, c_ctx, cache_attn_k, cache_attn_v, state_s5, state_gla, norm_mix, norm_ffn, w_mod, b_mod, w_router, ffn_w_gate, ffn_w_up, ffn_w_down, ab_w_in, ab_w_out, attn_q_norm, attn_k_norm, s5_lambda_re, s5_lambda_im, s5_log_dt, s5_b_re, s5_b_im, s5_c_re, s5_c_im, s5_d, s5_w_glu, s5_b_glu, gla_w_in, gla_w_gate_up, gla_b_gate, gla_o_norm, gla_w_out, norm_out):
    batch, seq, d = x_prompt.shape
    n_dec, ts, _ = x_sample.shape
    assert batch * seq == ts, "context batch must flatten to one latent-length sequence"
    depth = w_mod.shape[0]
    assert depth == 2 and ab_w_in.shape[0] == 1 and gla_w_in.shape[0] == 1
    ns = n_dec + 1
    ntok = ns * ts
    n_dec_tok = n_dec * ts
    ne = w_router.shape[2]
    tm = min(TOKEN_TILE, seq)
    tps = ts // tm
    qw = ATTN_HEADS * HEAD_DIM
    kw = ATTN_KV_HEADS * HEAD_DIM
    s5w = d // 2
    s5g = s5w // S5_GROUP_CH
    s5n = s5_lambda_re.shape[-1]
    past = cache_attn_k.shape[2]

    x_dec = x_sample.reshape(n_dec_tok, d)
    x_ctx = x_prompt.reshape(batch * seq, d)

    rows = -(-ns // 8) * 8
    cond = jnp.concatenate([c, c_ctx[None, :], jnp.zeros((rows - ns, d), F32)], axis=0)
    mod = _modulation(cond, w_mod, b_mod)[:, :ns].reshape(depth, ns, 6, d)
    mod = jnp.pad(mod, ((0, 0), (0, 0), (0, 2), (0, 0)))

    wr_pad = jnp.pad(w_router, ((0, 0), (0, 0), (0, LANES - ne)))
    wg_bf = _split_cast_cols(ffn_w_gate, MOE_F_SUB)
    wu_bf = _split_cast_cols(ffn_w_up, MOE_F_SUB)
    wd_bf = _split_cast_rows(ffn_w_down, MOE_F_SUB)

    def moe_layer(xin, aff, layer):
        idx, gates = _route(aff, n_dec=n_dec, ts=ts, batch=batch, seq=seq, ne=ne)
        out = _moe(xin.reshape(ns, ts, d), idx, gates, mod[layer], norm_ffn[layer:layer + 1],
                   wg_bf, wu_bf, wd_bf, layer=layer, nf=MOE_F_PARTS)
        return out.reshape(ntok, d)

    proj, u = _inproj_ab(x_dec, x_ctx, mod[0], norm_mix[0:1], ab_w_in[0].astype(BF16),
                         n_qkv=qw + 2 * kw, tps=tps, tm=tm)
    cos_tab, sin_tab = _rope_tables(ts)
    q_bf, k_bf, k_nrm, v_bf = _qkprep(proj, cos_tab, sin_tab, attn_q_norm[0], attn_k_norm[0],
                                      ts=ts, tm=tm, n_rope_seq=n_dec)
    keys = jnp.concatenate([cache_attn_k[:, 0].reshape(n_dec, past, kw).astype(BF16),
                            k_bf[:n_dec_tok].reshape(n_dec, ts, kw)], axis=1)
    vals = jnp.concatenate([cache_attn_v[:, 0].reshape(n_dec, past, kw).astype(BF16),
                            v_bf[:n_dec_tok].reshape(n_dec, ts, kw)], axis=1)
    o_attn = _attention(q_bf, keys.transpose(0, 2, 1), vals, None, row0=0, t=ts, tq=min(256, ts))
    o_attn = _attention(q_bf, k_bf[n_dec_tok:].reshape(batch, seq, kw).transpose(0, 2, 1),
                        v_bf[n_dec_tok:].reshape(batch, seq, kw), o_attn,
                        row0=n_dec_tok, t=seq, tq=min(256, seq))

    mats = _s5_matrices(s5_lambda_re[0], s5_lambda_im[0], s5_log_dt[0], s5_b_re[0], s5_b_im[0],
                        s5_c_re[0], s5_c_im[0], s5_d[0])
    h0 = state_s5[:, 0].transpose(2, 1, 4, 0, 3).reshape(s5g, 4, n_dec, s5n)
    h0 = jnp.pad(h0, ((0, 0), (0, 0), (0, 0), (0, LANES - s5n)))
    y_dec, _ = _s5(_s5_to_chunks(u[:n_dec_tok], n_dec, ts, s5g), *mats, h0,
                   nb=n_dec, nc=ts // S5_CHUNK)
    y_ctx, hfin = _s5(_s5_to_chunks(u[n_dec_tok:], batch, seq, s5g), *mats,
                      jnp.zeros((s5g, 4, batch, LANES), F32), nb=batch, nc=seq // S5_CHUNK)
    y = jnp.concatenate([_s5_from_chunks(y_dec, n_dec, ts), _s5_from_chunks(y_ctx, batch, seq)], axis=0)

    w_out = ab_w_out[0].astype(BF16)
    x, aff = _ab_post(y, o_attn, x_dec, x_ctx, mod[0], s5_w_glu[0].astype(BF16), s5_b_glu[0][None, :],
                      w_out[:qw], w_out[qw:], norm_ffn[0:1], wr_pad[0], tps=tps, tm=tm, ne=ne)
    x = moe_layer(x, aff, 0)

    nh = GLA_HEADS
    dk = d // 2 // nh
    dv = d // nh
    main_w = 2 * nh * dk + 2 * nh * dv
    w_gla = gla_w_in[0]
    wl = jnp.pad(w_gla[:, main_w:], ((0, 0), (0, LANES - 2 * GLA_RANK))).astype(BF16)
    wgu = jnp.zeros((LANES, 2 * nh * dk), F32)
    wgu = wgu.at[:GLA_RANK, :nh * dk].set(gla_w_gate_up[0, 0])
    wgu = wgu.at[GLA_RANK:2 * GLA_RANK, nh * dk:].set(gla_w_gate_up[0, 1])
    bgu = jnp.concatenate([gla_b_gate[0, 0], gla_b_gate[0, 1]])[None, :]
    qk1, v1, r1, gates1 = _inproj_gla(x, mod[1], norm_mix[1:2], w_gla[:, :main_w].astype(BF16), wl, wgu, bgu,
                                      widths=(2 * nh * dk, nh * dv, nh * dv), tps=tps, tm=tm)
    s0t = state_gla[:, 0].swapaxes(-1, -2)
    o_f, o_b, _ = _gla(qk1, v1, gates1, s0t, None, row0=0, nb=n_dec, t=ts, tb=min(512, ts), dk=dk, dv=dv)
    o_f, o_b, sfin = _gla(qk1, v1, gates1, jnp.zeros((batch, 2, nh, dv, dk), F32), (o_f, o_b),
                          row0=n_dec_tok, nb=batch, t=seq, tb=min(512, seq), dk=dk, dv=dv)
    x, aff = _gla_post(o_f, o_b, r1, x, mod[1], gla_o_norm[0][None, :], gla_w_out[0].astype(BF16),
                       norm_ffn[1:2], wr_pad[1], tps=tps, tm=tm, ne=ne)
    x = moe_layer(x, aff, 1)

    gain = norm_out[None, :]
    y_sample = _final_norm(x, gain, row0=0, nrows=n_dec_tok, tm=tm).reshape(n_dec, ts, d)
    y_prompt = _final_norm(x, gain, row0=n_dec_tok, nrows=batch * seq, tm=tm).reshape(batch, seq, d)
    new_k = k_nrm[n_dec_tok:].reshape(batch, 1, seq, ATTN_KV_HEADS, HEAD_DIM)
    new_v = proj[n_dec_tok:, qw + kw:qw + 2 * kw].reshape(batch, 1, seq, ATTN_KV_HEADS, HEAD_DIM)
    new_s5 = hfin[..., :s5n].reshape(s5g, 2, 2, batch, s5n).transpose(3, 1, 0, 4, 2)[:, None]
    new_gla = sfin.swapaxes(-1, -2)[:, None]
    return (y_prompt, y_sample, new_k, new_v, new_s5, new_gla)
```

```python
import functools
import math

import jax
import jax.numpy as jnp
from jax import lax
from jax.experimental import pallas as pl
from jax.experimental.pallas import tpu as pltpu

F32 = jnp.float32
BF16 = jnp.bfloat16

RMS_EPS = 1e-6
GRID_W = 64
ROPE_THETA = 10000.0
ATTN_HEADS = 8
ATTN_KV_HEADS = 2
HEAD_DIM = 64
S5_GROUP_CH = 16
S5_CHUNK = 16
GLA_HEADS = 4
GLA_RANK = 16
GLA_GATE_NORM = 16.0
GLA_CHUNK = 64
EC_CAPACITY = 2
MOE_F_SUB = 256
MOE_F_PARTS = 2
LANES = 128
TOKEN_TILE = 512
VMEM_LIMIT = 56 * 1024 * 1024


def _cparams(*sem):
    return pltpu.CompilerParams(dimension_semantics=sem, vmem_limit_bytes=VMEM_LIMIT)


def _dot(a, b):
    return jnp.dot(a, b, preferred_element_type=F32)


def _dot_nt(a, b):
    return lax.dot_general(a, b, (((1,), (1,)), ((), ())), preferred_element_type=F32)


def _split(a):
    hi = a.astype(BF16)
    lo = (a - hi.astype(F32)).astype(BF16)
    return hi, lo


def _dot3(a, b):
    ah, al = _split(a)
    bh, bl = _split(b)
    return _dot(ah, bh) + _dot(al, bh) + _dot(ah, bl)


def _dot2(a, b_exact):
    ah, al = _split(a)
    return _dot(ah, b_exact) + _dot(al, b_exact)


def _sigmoid(x):
    return 1.0 / (1.0 + jnp.exp(-x))


def _silu(x):
    return x * _sigmoid(x)


def _gelu_tanh(x):
    c = math.sqrt(2.0 / math.pi)
    return 0.5 * x * (1.0 + jnp.tanh(c * (x + 0.044715 * (x * x * x))))


def _log_sigmoid(z):
    return jnp.minimum(z, 0.0) - jnp.log1p(jnp.exp(-jnp.abs(z)))


def _rms_mod(x, gain, scale, shift):
    ms = jnp.mean(x * x, axis=-1, keepdims=True)
    return (x * lax.rsqrt(ms + RMS_EPS)) * gain * (1.0 + scale) + shift


def _router_probs(xn, mod, gain, wr, ne):
    h = _rms_mod(xn, gain, mod[4:5], mod[3:4])
    logits = _dot3(h, wr)
    lane = lax.broadcasted_iota(jnp.int32, logits.shape, 1)
    logits = jnp.where(lane < ne, logits, -jnp.inf)
    m = jnp.max(logits, axis=-1, keepdims=True)
    p = jnp.exp(logits - m)
    return p / jnp.sum(p, axis=-1, keepdims=True)


def _mod_kernel(c_ref, w_ref, b_ref, o_ref):
    c = c_ref[...]
    o_ref[0] = _dot3(_silu(c), w_ref[0]) + b_ref[0]


def _modulation(cond, w_mod, b_mod):
    depth, d, n = w_mod.shape
    r = cond.shape[0]
    tn = n // 4
    return pl.pallas_call(
        _mod_kernel,
        name="modulation",
        grid=(depth, n // tn),
        in_specs=[
            pl.BlockSpec((r, d), lambda l, j: (0, 0)),
            pl.BlockSpec((1, d, tn), lambda l, j: (l, 0, j)),
            pl.BlockSpec((1, 1, tn), lambda l, j: (l, 0, j)),
        ],
        out_specs=pl.BlockSpec((1, r, tn), lambda l, j: (l, 0, j)),
        out_shape=jax.ShapeDtypeStruct((depth, r, n), F32),
        compiler_params=_cparams("parallel", "parallel"),
    )(cond, w_mod, b_mod.reshape(depth, 1, n))


def _two_group_specs(tm, d, nd_tiles):
    return [pl.BlockSpec((tm, d), lambda i: (jnp.minimum(i, nd_tiles - 1), 0)),
            pl.BlockSpec((tm, d), lambda i: (jnp.maximum(i - nd_tiles, 0), 0))]


def _two_group_tile(xd_ref, xc_ref, nd_tiles):
    return jnp.where(pl.program_id(0) < nd_tiles, xd_ref[...], xc_ref[...])


def _to_step_major(tok_ref, out_ref):
    nq, steps, nrow, _ = out_ref.shape
    for q in range(nq):
        for i in range(steps):
            out_ref[q, i] = tok_ref[q, pl.ds(i, nrow, stride=steps), :].astype(out_ref.dtype)


def _from_step_major(in_ref, tok_ref):
    nq, steps, nrow, _ = in_ref.shape
    for q in range(nq):
        for i in range(steps):
            tok_ref[q, pl.ds(i, nrow, stride=steps), :] = in_ref[q, i].astype(F32)


def _inproj_ab_kernel(xd_ref, xc_ref, mod_ref, g_ref, w_ref, qkv_ref, u_ref, scr_ref, *, nd_tiles):
    m = mod_ref[0]
    x = _two_group_tile(xd_ref, xc_ref, nd_tiles)
    h = _rms_mod(x, g_ref[...], m[1:2], m[0:1]).astype(BF16)
    nq = qkv_ref.shape[1]
    qkv_ref[...] = _dot(h, w_ref[:, :nq])
    u = _dot(h, w_ref[:, nq:])
    for q in range(scr_ref.shape[0]):
        scr_ref[q] = u[:, q * LANES:(q + 1) * LANES]
    _to_step_major(scr_ref, u_ref)


def _inproj_ab(x_dec, x_ctx, mod, gain, w, *, n_qkv, tps, tm):
    d = x_dec.shape[1]
    nd_tiles = x_dec.shape[0] // tm
    ntok = x_dec.shape[0] + x_ctx.shape[0]
    n = w.shape[1]
    uw = n - n_qkv
    return pl.pallas_call(
        functools.partial(_inproj_ab_kernel, nd_tiles=nd_tiles),
        name="inproj_ab",
        grid=(ntok // tm,),
        in_specs=_two_group_specs(tm, d, nd_tiles) + [
            pl.BlockSpec((1, 8, d), lambda i: (i // tps, 0, 0)),
            pl.BlockSpec((1, d), lambda i: (0, 0)),
            pl.BlockSpec((d, n), lambda i: (0, 0)),
        ],
        out_specs=[pl.BlockSpec((tm, n_qkv), lambda i: (i, 0)),
                   pl.BlockSpec((uw // LANES, S5_CHUNK, tm // S5_CHUNK, LANES), lambda i: (0, 0, i, 0))],
        out_shape=[jax.ShapeDtypeStruct((ntok, n_qkv), F32),
                   jax.ShapeDtypeStruct((uw // LANES, S5_CHUNK, ntok // S5_CHUNK, LANES), BF16)],
        scratch_shapes=[pltpu.VMEM((uw // LANES, tm, LANES), F32)],
        compiler_params=_cparams("parallel"),
    )(x_dec, x_ctx, mod, gain, w)


def _inproj_gla_kernel(x_ref, mod_ref, g_ref, w_ref, wl_ref, wg_ref, bg_ref, qk_ref, v_ref, r_ref, gate_ref):
    m = mod_ref[0]
    h = _rms_mod(x_ref[...], g_ref[...], m[1:2], m[0:1]).astype(BF16)
    c0 = 0
    for ref in (qk_ref, v_ref, r_ref):
        w = ref.shape[1]
        for s in range(0, w, 512):
            ref[:, s:s + 512] = _dot(h, w_ref[:, c0 + s:c0 + s + 512]).astype(ref.dtype)
        c0 += w
    low = _dot(h, wl_ref[...])
    z = _dot3(low, wg_ref[...]) + bg_ref[...]
    gate_ref[...] = (_log_sigmoid(z) / GLA_GATE_NORM).astype(gate_ref.dtype)


def _inproj_gla(x, mod, gain, w, wl, wg, bg, *, widths, tps, tm):
    ntok, d = x.shape
    n = w.shape[1]
    ng = wg.shape[1]
    wqk, wv, wr = widths
    row = lambda width: pl.BlockSpec((tm, width), lambda i: (i, 0))
    return pl.pallas_call(
        _inproj_gla_kernel,
        name="inproj_gla",
        grid=(ntok // tm,),
        in_specs=[
            pl.BlockSpec((tm, d), lambda i: (i, 0)),
            pl.BlockSpec((1, 8, d), lambda i: (i // tps, 0, 0)),
            pl.BlockSpec((1, d), lambda i: (0, 0)),
            pl.BlockSpec((d, n), lambda i: (0, 0)),
            pl.BlockSpec((d, LANES), lambda i: (0, 0)),
            pl.BlockSpec((LANES, ng), lambda i: (0, 0)),
            pl.BlockSpec((1, ng), lambda i: (0, 0)),
        ],
        out_specs=[row(wqk), row(wv), row(wr), row(ng)],
        out_shape=[jax.ShapeDtypeStruct((ntok, wqk), BF16),
                   jax.ShapeDtypeStruct((ntok, wv), BF16),
                   jax.ShapeDtypeStruct((ntok, wr), BF16),
                   jax.ShapeDtypeStruct((ntok, ng), BF16)],
        compiler_params=_cparams("parallel"),
    )(x, mod, gain, w, wl, wg, bg)


def _qkprep_kernel(q_ref, k_ref, v_ref, cos_ref, sin_ref, qg_ref, kg_ref, oq_ref, ok_ref,
                   qo_ref, ko_ref, kn_ref, vo_ref):
    cos = cos_ref[0]
    sin = sin_ref[0]

    def prep(x, gain, ones):
        width = x.shape[1]
        ssq = _dot2(x * x, ones)
        xn = x * lax.rsqrt(ssq * (1.0 / HEAD_DIM) + RMS_EPS) * gain
        lane = lax.broadcasted_iota(jnp.int32, x.shape, 1)
        first = (lane & 16) == 0
        partner = jnp.where(first, pltpu.roll(xn, width - 16, 1), pltpu.roll(xn, 16, 1))
        rep = width // LANES
        cosw = jnp.concatenate([cos] * rep, axis=1) if rep > 1 else cos
        sinw = jnp.concatenate([sin] * rep, axis=1) if rep > 1 else sin
        return xn, xn * cosw + partner * sinw

    _, qr = prep(q_ref[...], qg_ref[...], oq_ref[...])
    qo_ref[...] = (qr * (HEAD_DIM ** -0.5)).astype(BF16)
    kn, kr = prep(k_ref[...], kg_ref[...], ok_ref[...])
    kn_ref[...] = kn
    ko_ref[...] = kr.astype(BF16)
    vo_ref[...] = v_ref[...].astype(BF16)


def _qkprep(proj, cos_tab, sin_tab, q_gain, k_gain, *, ts, tm, n_rope_seq):
    ntok = proj.shape[0]
    qw = ATTN_HEADS * HEAD_DIM
    kw = ATTN_KV_HEADS * HEAD_DIM
    tps = ts // tm

    def head_ones(width):
        a = jnp.arange(width) // HEAD_DIM
        return (a[:, None] == a[None, :]).astype(BF16)

    def tab_map(i):
        return (jnp.minimum(i // tps, n_rope_seq) // n_rope_seq, i % tps, 0)

    return pl.pallas_call(
        _qkprep_kernel,
        name="qkprep",
        grid=(ntok // tm,),
        in_specs=[
            pl.BlockSpec((tm, qw), lambda i: (i, 0)),
            pl.BlockSpec((tm, kw), lambda i: (i, qw // kw)),
            pl.BlockSpec((tm, kw), lambda i: (i, qw // kw + 1)),
            pl.BlockSpec((1, tm, LANES), tab_map),
            pl.BlockSpec((1, tm, LANES), tab_map),
            pl.BlockSpec((1, qw), lambda i: (0, 0)),
            pl.BlockSpec((1, kw), lambda i: (0, 0)),
            pl.BlockSpec((qw, qw), lambda i: (0, 0)),
            pl.BlockSpec((kw, kw), lambda i: (0, 0)),
        ],
        out_specs=[
            pl.BlockSpec((tm, qw), lambda i: (i, 0)),
            pl.BlockSpec((tm, kw), lambda i: (i, 0)),
            pl.BlockSpec((tm, kw), lambda i: (i, 0)),
            pl.BlockSpec((tm, kw), lambda i: (i, 0)),
        ],
        out_shape=[
            jax.ShapeDtypeStruct((ntok, qw), BF16),
            jax.ShapeDtypeStruct((ntok, kw), BF16),
            jax.ShapeDtypeStruct((ntok, kw), F32),
            jax.ShapeDtypeStruct((ntok, kw), BF16),
        ],
        compiler_params=_cparams("parallel"),
    )(proj, proj, proj, cos_tab, sin_tab,
      jnp.tile(q_gain, ATTN_HEADS).reshape(1, qw), jnp.tile(k_gain, ATTN_KV_HEADS).reshape(1, kw),
      head_ones(qw), head_ones(kw))


def _rope_tables(t):
    nfreq = HEAD_DIM // 4
    inv = ROPE_THETA ** (-jnp.arange(nfreq, dtype=F32) / nfreq)
    pos = jnp.arange(t)
    row = (pos // GRID_W).astype(F32)[:, None] * inv[None, :]
    col = (pos % GRID_W).astype(F32)[:, None] * inv[None, :]
    cos = jnp.concatenate([jnp.cos(row), jnp.cos(row), jnp.cos(col), jnp.cos(col)], axis=1)
    sin = jnp.concatenate([-jnp.sin(row), jnp.sin(row), -jnp.sin(col), jnp.sin(col)], axis=1)
    rep = LANES // HEAD_DIM
    cos = jnp.tile(cos, (1, rep))
    sin = jnp.tile(sin, (1, rep))
    return (jnp.stack([cos, jnp.ones_like(cos)]), jnp.stack([sin, jnp.zeros_like(sin)]))


def _attn_kernel(q_ref, kt_ref, v_ref, *rest):
    o_ref = rest[-1]
    q = q_ref[...]
    v = v_ref[0]
    group = ATTN_HEADS // ATTN_KV_HEADS
    outs = []
    for h in range(ATTN_HEADS):
        kv = h // group
        qh = q[:, h * HEAD_DIM:(h + 1) * HEAD_DIM]
        kt = kt_ref[0, kv * HEAD_DIM:(kv + 1) * HEAD_DIM, :]
        s = _dot(qh, kt)
        p = jnp.exp(s - jnp.max(s, axis=-1, keepdims=True))
        l = jnp.sum(p, axis=-1, keepdims=True)
        o = _dot(p.astype(BF16), v)
        outs.append(o[:, kv * HEAD_DIM:(kv + 1) * HEAD_DIM] / l)
    o_ref[...] = jnp.concatenate(outs, axis=1).astype(BF16)


def _attention(q, keys_t, vals, prev, *, row0, t, tq):
    b, kw, s = keys_t.shape
    ntok, qw = q.shape
    nq = t // tq
    rb0 = row0 // tq
    extra = [] if prev is None else [prev]
    return pl.pallas_call(
        _attn_kernel,
        name="attention",
        grid=(b, nq),
        in_specs=[
            pl.BlockSpec((tq, qw), lambda bi, i: (rb0 + bi * nq + i, 0)),
            pl.BlockSpec((1, kw, s), lambda bi, i: (bi, 0, 0)),
            pl.BlockSpec((1, s, kw), lambda bi, i: (bi, 0, 0)),
        ] + [pl.BlockSpec(memory_space=pl.ANY)] * len(extra),
        out_specs=pl.BlockSpec((tq, qw), lambda bi, i: (rb0 + bi * nq + i, 0)),
        out_shape=jax.ShapeDtypeStruct((ntok, qw), BF16),
        input_output_aliases={3: 0} if extra else {},
        compiler_params=_cparams("parallel", "parallel"),
    )(q, keys_t, vals, *extra)


def _s5_kernel(u_ref, e_ref, m_ref, f_ref, lam_ref, h0_ref, y_ref, hfin_ref, xg_ref, s_ref, hin_ref, *, p, seqs):
    steps, nr = u_ref.shape[1], u_ref.shape[2]
    gpb = LANES // p
    lane_blk = lax.broadcasted_iota(jnp.int32, (1, LANES), 1) // p
    y_ref[...] = jnp.zeros(y_ref.shape, y_ref.dtype)

    def group(g8, carry):
        halves = []
        for half in range(steps // gpb):
            acc = jnp.zeros((nr, LANES), F32)
            for j in range(gpb):
                slab = u_ref[0, half * gpb + j].astype(F32)
                rolled = pltpu.roll(slab, (p * j + LANES - p * g8) & (LANES - 1), 1)
                acc = jnp.where(lane_blk == j, rolled, acc)
            halves.append(acc)
        xg_ref[...] = jnp.concatenate(halves, axis=1).astype(BF16)
        xg = xg_ref[...]
        eh, el = _split(e_ref[g8])
        ends = _dot(xg, eh) + _dot(xg, el)
        for k in range(4):
            s_ref[k] = ends[:, k * LANES:(k + 1) * LANES]
        lam = lam_ref[g8]

        for nb, nc, row0, carries in seqs:
            lfr = jnp.broadcast_to(lam[0:1], (nb, LANES))
            lfi = jnp.broadcast_to(lam[1:2], (nb, LANES))
            lbr = jnp.broadcast_to(lam[2:3], (nb, LANES))
            lbi = jnp.broadcast_to(lam[3:4], (nb, LANES))

            def step(c, carry, nb=nb, nc=nc, row0=row0, lfr=lfr, lfi=lfi, lbr=lbr, lbi=lbi):
                hfr, hfi, hbr, hbi = carry
                rf = pl.ds(row0 + c, nb, stride=nc)
                rb = pl.ds(row0 + nc - 1 - c, nb, stride=nc)
                hin_ref[0, rf, :] = hfr
                hin_ref[1, rf, :] = hfi
                hin_ref[2, rb, :] = hbr
                hin_ref[3, rb, :] = hbi
                sfr = s_ref[0, rf, :]
                sfi = s_ref[1, rf, :]
                sbr = s_ref[2, rb, :]
                sbi = s_ref[3, rb, :]
                return (lfr * hfr - lfi * hfi + sfr, lfr * hfi + lfi * hfr + sfi,
                        lbr * hbr - lbi * hbi + sbr, lbr * hbi + lbi * hbr + sbi)

            if carries:
                h0 = h0_ref[g8]
                init = (h0[0], h0[1], h0[2], h0[3])
            else:
                init = (jnp.zeros((nb, LANES), F32),) * 4
            fin = lax.fori_loop(0, nc, step, init)
            if not carries:
                for k in range(4):
                    hfin_ref[g8, k] = fin[k]

        mh, ml = _split(m_ref[g8])
        y = _dot(xg, mh) + _dot(xg, ml)
        for k in range(4):
            y = y + _dot3(hin_ref[k], f_ref[g8, k * LANES:(k + 1) * LANES, :])
        for half in range(steps // gpb):
            piece = y[:, half * LANES:(half + 1) * LANES]
            for j in range(gpb):
                rolled = pltpu.roll(piece, (p * g8 + LANES - p * j) & (LANES - 1), 1)
                i = half * gpb + j
                y_ref[0, i] = jnp.where(lane_blk == g8, rolled.astype(y_ref.dtype), y_ref[0, i])
        return carry

    lax.fori_loop(0, gpb, group, 0)


def _s5(u5, e_mat, m_mat, f_mat, lam_l, h0, *, p, seqs):
    nq, steps, nr, _ = u5.shape
    gpb = LANES // p
    w = steps * p
    sw = 4 * LANES
    nb_carry = h0.shape[2]
    nb_free = [s[0] for s in seqs if not s[3]][0]
    once = pl.Buffered(1)
    return pl.pallas_call(
        functools.partial(_s5_kernel, p=p, seqs=seqs),
        name="s5",
        grid=(nq,),
        in_specs=[
            pl.BlockSpec((1, steps, nr, LANES), lambda q: (q, 0, 0, 0), pipeline_mode=once),
            pl.BlockSpec((gpb, w, sw), lambda q: (q, 0, 0), pipeline_mode=once),
            pl.BlockSpec((gpb, w, w), lambda q: (q, 0, 0), pipeline_mode=once),
            pl.BlockSpec((gpb, sw, w), lambda q: (q, 0, 0), pipeline_mode=once),
            pl.BlockSpec((gpb, 8, LANES), lambda q: (q, 0, 0)),
            pl.BlockSpec((gpb, 4, nb_carry, LANES), lambda q: (q, 0, 0, 0)),
        ],
        out_specs=[pl.BlockSpec((1, steps, nr, LANES), lambda q: (q, 0, 0, 0), pipeline_mode=once),
                   pl.BlockSpec((gpb, 4, nb_free, LANES), lambda q: (q, 0, 0, 0))],
        out_shape=[jax.ShapeDtypeStruct(u5.shape, BF16),
                   jax.ShapeDtypeStruct((nq * gpb, 4, nb_free, LANES), F32)],
        scratch_shapes=[pltpu.VMEM((nr, w), BF16), pltpu.VMEM((4, nr, LANES), F32), pltpu.VMEM((4, nr, LANES), F32)],
        compiler_params=_cparams("parallel"),
    )(u5, e_mat, m_mat, f_mat, lam_l, h0)


def _s5_matrices(lam_re, lam_im, log_dt, b_re, b_im, c_re, c_im, d_skip):
    hi = lax.Precision.HIGHEST
    L = S5_CHUNK
    _, g, n = lam_re.shape
    p = b_re.shape[-1]
    dt = jnp.exp(log_dt)[..., None]
    mag = jnp.exp(lam_re * dt)
    lbr, lbi = mag * jnp.cos(lam_im * dt), mag * jnp.sin(lam_im * dt)
    den = lam_re * lam_re + lam_im * lam_im
    cr = ((lbr - 1.0) * lam_re + lbi * lam_im) / den
    ci = (lbi * lam_re - (lbr - 1.0) * lam_im) / den
    bbr = cr[..., None] * b_re - ci[..., None] * b_im
    bbi = cr[..., None] * b_im + ci[..., None] * b_re
    pr, pi = [jnp.ones_like(lbr)], [jnp.zeros_like(lbr)]
    for _ in range(L):
        pr, pi = pr + [pr[-1] * lbr - pi[-1] * lbi], pi + [pr[-1] * lbi + pi[-1] * lbr]
    pwr, pwi = jnp.stack(pr, axis=2), jnp.stack(pi, axis=2)
    cpr = c_re[:, :, None] * pwr[:, :, :, None, :] - c_im[:, :, None] * pwi[:, :, :, None, :]
    cpi = c_re[:, :, None] * pwi[:, :, :, None, :] + c_im[:, :, None] * pwr[:, :, :, None, :]
    kern = (jnp.einsum('dgkpn,dgnq->dgkpq', cpr[:, :, :L], bbr, precision=hi)
            - jnp.einsum('dgkpn,dgnq->dgkpq', cpi[:, :, :L], bbi, precision=hi))
    step = jnp.arange(L)
    lag_f = step[None, :] - step[:, None]
    lag_b = -lag_f

    def toeplitz(kd, lag):
        t = jnp.take(kd, jnp.clip(lag, 0, L - 1), axis=1)
        t = jnp.where((lag >= 0)[None, :, :, None, None], t, 0.0)
        return t.transpose(0, 1, 4, 2, 3).reshape(g, L * p, L * p)

    eye_l = jnp.eye(L, dtype=F32)
    dmat = jnp.einsum('ji,gq,qp->gjqip', eye_l, d_skip, jnp.eye(p, dtype=F32)).reshape(g, L * p, L * p)
    m_mat = toeplitz(kern[0], lag_f) + toeplitz(kern[1], lag_b) + dmat

    def e_block(pw_r, pw_i, b_r, b_i):
        er = pw_r[:, :, None, :] * b_r.transpose(0, 2, 1)[:, None] - pw_i[:, :, None, :] * b_i.transpose(0, 2, 1)[:, None]
        ei = pw_r[:, :, None, :] * b_i.transpose(0, 2, 1)[:, None] + pw_i[:, :, None, :] * b_r.transpose(0, 2, 1)[:, None]
        return er.reshape(g, L * p, n), ei.reshape(g, L * p, n)

    def pad_cols(a):
        return jnp.pad(a, ((0, 0), (0, 0), (0, LANES - n)))

    efr, efi = e_block(pwr[0][:, L - 1 - step], pwi[0][:, L - 1 - step], bbr[0], bbi[0])
    ebr, ebi = e_block(pwr[1][:, step], pwi[1][:, step], bbr[1], bbi[1])
    e_mat = jnp.concatenate([pad_cols(efr), pad_cols(efi), pad_cols(ebr), pad_cols(ebi)], axis=2)

    def f_block(c_r, c_i):
        fr = c_r.transpose(0, 3, 1, 2).reshape(g, n, L * p)
        fi = -c_i.transpose(0, 3, 1, 2).reshape(g, n, L * p)
        return fr, fi

    def pad_rows(a):
        return jnp.pad(a, ((0, 0), (0, LANES - n), (0, 0)))

    ffr, ffi = f_block(cpr[0][:, step + 1], cpi[0][:, step + 1])
    fbr, fbi = f_block(cpr[1][:, L - step], cpi[1][:, L - step])
    f_mat = jnp.concatenate([pad_rows(ffr), pad_rows(ffi), pad_rows(fbr), pad_rows(fbi)], axis=1)

    lam_l = jnp.stack([pwr[0, :, L], pwi[0, :, L], pwr[1, :, L], pwi[1, :, L]], axis=1)
    lam_l = jnp.pad(lam_l, ((0, 0), (0, 4), (0, LANES - n)))
    return e_mat, m_mat, f_mat, lam_l


def _ab_post_kernel(y_ref, oa_ref, xd_ref, xc_ref, mod_ref, wglu_ref, bglu_ref, woa_ref, wos_ref, gffn_ref,
                    wr_ref, xo_ref, aff_ref, ytok_ref, *, ne, nd_tiles):
    m = mod_ref[0]
    half = wos_ref.shape[0]
    _from_step_major(y_ref, ytok_ref)
    y = jnp.concatenate([ytok_ref[q] for q in range(ytok_ref.shape[0])], axis=1)
    glu = _dot(_gelu_tanh(y).astype(BF16), wglu_ref[...]) + bglu_ref[...]
    o_s5 = glu[:, :half] * _sigmoid(glu[:, half:])
    out = _dot(oa_ref[...], woa_ref[...]) + _dot(o_s5.astype(BF16), wos_ref[...])
    xn = _two_group_tile(xd_ref, xc_ref, nd_tiles) + m[2:3] * out
    xo_ref[...] = xn
    aff_ref[...] = _router_probs(xn, m, gffn_ref[...], wr_ref[...], ne)


def _ab_post(y, o_attn, x_dec, x_ctx, mod, w_glu, b_glu, w_out_a, w_out_s, g_ffn, w_router, *, tps, tm, ne):
    d = x_dec.shape[1]
    nd_tiles = x_dec.shape[0] // tm
    ntok = x_dec.shape[0] + x_ctx.shape[0]
    nq, steps, _, _ = y.shape
    full = lambda shape: pl.BlockSpec(shape, lambda i: (0,) * len(shape))
    return pl.pallas_call(
        functools.partial(_ab_post_kernel, ne=ne, nd_tiles=nd_tiles),
        name="ab_post",
        grid=(ntok // tm,),
        scratch_shapes=[pltpu.VMEM((nq, tm, LANES), F32)],
        in_specs=[
            pl.BlockSpec((nq, steps, tm // steps, LANES), lambda i: (0, 0, i, 0)),
            pl.BlockSpec((tm, o_attn.shape[1]), lambda i: (i, 0)),
        ] + _two_group_specs(tm, d, nd_tiles) + [
            pl.BlockSpec((1, 8, d), lambda i: (i // tps, 0, 0)),
            full(w_glu.shape), full(b_glu.shape), full(w_out_a.shape), full(w_out_s.shape),
            full(g_ffn.shape), full(w_router.shape),
        ],
        out_specs=[pl.BlockSpec((tm, d), lambda i: (i, 0)),
                   pl.BlockSpec((tm, LANES), lambda i: (i, 0))],
        out_shape=[jax.ShapeDtypeStruct((ntok, d), F32),
                   jax.ShapeDtypeStruct((ntok, LANES), F32)],
        compiler_params=_cparams("parallel"),
    )(y, o_attn, x_dec, x_ctx, mod, w_glu, b_glu, w_out_a, w_out_s, g_ffn, w_router)


def _gla_post_kernel(of_ref, ob_ref, r_ref, x_ref, mod_ref, gn_ref, wo_ref, gffn_ref, wr_ref, xo_ref, aff_ref,
                     *, ne):
    m = mod_ref[0]
    o = of_ref[...].astype(F32) + ob_ref[...].astype(F32)
    dv = gn_ref.shape[1]
    parts = []
    for h in range(GLA_HEADS):
        oh = o[:, h * dv:(h + 1) * dv]
        ms = jnp.mean(oh * oh, axis=-1, keepdims=True)
        parts.append(oh * lax.rsqrt(ms + RMS_EPS) * gn_ref[...])
    on = jnp.concatenate(parts, axis=1) * _silu(r_ref[...].astype(F32))
    out = _dot(on.astype(BF16), wo_ref[...])
    xn = x_ref[...] + m[2:3] * out
    xo_ref[...] = xn
    aff_ref[...] = _router_probs(xn, m, gffn_ref[...], wr_ref[...], ne)


def _gla_post(o_fwd, o_bwd, r, x, mod, g_norm, w_out, g_ffn, w_router, *, tps, tm, ne):
    ntok, d = x.shape
    vw = o_fwd.shape[1]
    full = lambda shape: pl.BlockSpec(shape, lambda i: (0,) * len(shape))
    return pl.pallas_call(
        functools.partial(_gla_post_kernel, ne=ne),
        name="gla_post",
        grid=(ntok // tm,),
        in_specs=[
            pl.BlockSpec((tm, vw), lambda i: (i, 0)),
            pl.BlockSpec((tm, vw), lambda i: (i, 0)),
            pl.BlockSpec((tm, vw), lambda i: (i, 0)),
            pl.BlockSpec((tm, d), lambda i: (i, 0)),
            pl.BlockSpec((1, 8, d), lambda i: (i // tps, 0, 0)),
            full(g_norm.shape), full(w_out.shape), full(g_ffn.shape), full(w_router.shape),
        ],
        out_specs=[pl.BlockSpec((tm, d), lambda i: (i, 0)),
                   pl.BlockSpec((tm, LANES), lambda i: (i, 0))],
        out_shape=[jax.ShapeDtypeStruct((ntok, d), F32),
                   jax.ShapeDtypeStruct((ntok, LANES), F32)],
        compiler_params=_cparams("parallel"),
    )(o_fwd, o_bwd, r, x, mod, g_norm, w_out, g_ffn, w_router)


def _gla_kernel(qf_ref, kf_ref, vf_ref, gf_ref, qb_ref, kb_ref, vb_ref, gb_ref, s0_ref, *rest,
                nchunk, dk, dv, nh):
    of_ref, ob_ref, sfin_ref, st_ref, qd_ref, ke_ref, tot_ref, oacc_ref = rest[-8:]
    j = pl.program_id(1)
    c = GLA_CHUNK
    tb = nchunk * c

    @pl.when(j == 0)
    def _():
        st_ref[...] = s0_ref[0]

    row = lax.broadcasted_iota(jnp.int32, (tb, tb), 0)
    col = lax.broadcasted_iota(jnp.int32, (tb, tb), 1)
    same = (row // c) == (col // c)
    same_b = same.astype(BF16)
    dirs = ((qf_ref, kf_ref, vf_ref, gf_ref, of_ref, same & (row >= col)),
            (qb_ref, kb_ref, vb_ref, gb_ref, ob_ref, same & (row <= col)))
    for d, (q_ref, k_ref, v_ref, g_ref, o_ref, tri) in enumerate(dirs):
        g = g_ref[...]
        cum = _dot(tri.astype(BF16), g)
        tot = _dot(same_b, g)
        k = k_ref[...].astype(F32)
        q_dec = (q_ref[...].astype(F32) * (dk ** -0.5) * jnp.exp(cum)).astype(BF16)
        k_inv = (k * jnp.exp(-cum)).astype(BF16)
        qd_ref[d] = q_dec
        ke_ref[d] = (k * jnp.exp(tot - cum)).astype(BF16)
        tot_ref[d] = tot
        for h in range(nh):
            ks = slice(h * dk, (h + 1) * dk)
            vs = slice(h * dv, (h + 1) * dv)
            scores = jnp.where(tri, _dot_nt(q_dec[:, ks], k_inv[:, ks]), 0.0)
            oacc_ref[d, :, vs] = _dot(scores.astype(BF16), v_ref[:, vs])

    def step(i, carry):
        for d, (_, _, v_ref, _, _, _) in enumerate(dirs):
            r0 = pl.multiple_of((i if d == 0 else nchunk - 1 - i) * c, c)
            decay = jnp.exp(tot_ref[d, pl.ds(r0, 1), :])
            for h in range(nh):
                ks = slice(h * dk, (h + 1) * dk)
                vs = slice(h * dv, (h + 1) * dv)
                st = st_ref[d, h]
                oacc_ref[d, pl.ds(r0, c), vs] += _dot_nt(qd_ref[d, pl.ds(r0, c), ks], st.astype(BF16))
                vt = v_ref[pl.ds(r0, c), vs].astype(F32).T.astype(BF16)
                st_ref[d, h] = st * decay[:, ks] + _dot(vt, ke_ref[d, pl.ds(r0, c), ks])
        return carry

    lax.fori_loop(0, nchunk, step, 0)
    of_ref[...] = oacc_ref[0].astype(of_ref.dtype)
    ob_ref[...] = oacc_ref[1].astype(ob_ref.dtype)

    @pl.when(j == pl.num_programs(1) - 1)
    def _():
        sfin_ref[0] = st_ref[...]


def _gla(qk, v, gates, s0t, prev, *, row0, nb, t, tb, dk, dv):
    h = GLA_HEADS
    ntok = qk.shape[0]
    nblk = t // tb
    rb0 = row0 // tb
    kw, vw = h * dk, h * dv
    fwd = lambda b, j: rb0 + b * nblk + j
    bwd = lambda b, j: rb0 + b * nblk + nblk - 1 - j

    def views(rows, gate_col):
        return [pl.BlockSpec((tb, kw), lambda b, j: (rows(b, j), 0)),
                pl.BlockSpec((tb, kw), lambda b, j: (rows(b, j), 1)),
                pl.BlockSpec((tb, vw), lambda b, j: (rows(b, j), 0)),
                pl.BlockSpec((tb, kw), lambda b, j: (rows(b, j), gate_col))]

    extra = [] if prev is None else list(prev)
    state_spec = pl.BlockSpec((1, 2, h, dv, dk), lambda b, j: (b, 0, 0, 0, 0))
    return pl.pallas_call(
        functools.partial(_gla_kernel, nchunk=tb // GLA_CHUNK, dk=dk, dv=dv, nh=h),
        name="gla",
        input_output_aliases={9: 0, 10: 1} if extra else {},
        grid=(nb, nblk),
        in_specs=views(fwd, 0) + views(bwd, 1) + [state_spec] + [pl.BlockSpec(memory_space=pl.ANY)] * len(extra),
        out_specs=[
            pl.BlockSpec((tb, vw), lambda b, j: (fwd(b, j), 0)),
            pl.BlockSpec((tb, vw), lambda b, j: (bwd(b, j), 0)),
            state_spec,
        ],
        out_shape=[jax.ShapeDtypeStruct((ntok, vw), BF16),
                   jax.ShapeDtypeStruct((ntok, vw), BF16),
                   jax.ShapeDtypeStruct((nb, 2, h, dv, dk), F32)],
        scratch_shapes=[pltpu.VMEM((2, h, dv, dk), F32), pltpu.VMEM((2, tb, kw), BF16),
                        pltpu.VMEM((2, tb, kw), BF16), pltpu.VMEM((2, tb, kw), F32),
                        pltpu.VMEM((2, tb, vw), F32)],
        compiler_params=_cparams("parallel", "arbitrary"),
    )(qk, qk, v, gates, qk, qk, v, gates, s0t, *extra)


def _moe_kernel(idx_ref, gsm_ref, x_ref, mod_ref, gffn_ref, wg_ref, wu_ref, wd_ref, o_ref,
                xs_ref, gbuf_ref, ya_ref, yb_ref, *, cap, ne, nf):
    e = pl.program_id(1)
    f = pl.program_id(2)
    m = mod_ref[0]
    g2 = m[5:6]
    nsub = wg_ref.shape[1]
    rows_per = cap // (nf * nsub)
    e_next = jnp.minimum(e + 1, ne - 1)
    e_prev = jnp.maximum(e - 1, 0)

    def gather_row(en, r):
        gbuf_ref[pl.ds(r, 1), :] = x_ref[0, pl.ds(idx_ref[0, en, 0, r], 1), :]

    def scatter_row(ep, y_ref, r):
        i = idx_ref[0, ep, 0, r]
        o_ref[0, pl.ds(i, 1), :] += y_ref[pl.ds(r, 1), :] * gsm_ref[0, ep, 0, r] * g2

    @pl.when((e == 0) & (f == 0))
    def _():
        o_ref[...] = x_ref[...]
        yb_ref[...] = jnp.zeros(yb_ref.shape, F32)

        def body(r, carry):
            gather_row(0, r)
            return carry
        lax.fori_loop(0, cap, body, 0, unroll=8)

    @pl.when(f == 0)
    def _():
        xs_ref[...] = _rms_mod(gbuf_ref[...], gffn_ref[...], m[4:5], m[3:4]).astype(BF16)

    def expert_part(y_cur, y_prev):
        @pl.when(f == 0)
        def _():
            y_cur[...] = jnp.zeros(y_cur.shape, F32)

        def sub_chunk(k, carry):
            r0 = pl.multiple_of((f * nsub + k) * rows_per, rows_per)
            for r in range(rows_per):
                gather_row(e_next, r0 + r)
                scatter_row(e_prev, y_prev, r0 + r)
            xs = xs_ref[...]
            hid = _silu(_dot(xs, wg_ref[0, k])) * _dot(xs, wu_ref[0, k])
            y_cur[...] += _dot(hid.astype(BF16), wd_ref[0, k])
            return carry

        lax.fori_loop(0, nsub, sub_chunk, 0)

        @pl.when((e == ne - 1) & (f == nf - 1))
        def _():
            def body(r, carry):
                scatter_row(e, y_cur, r)
                return carry
            lax.fori_loop(0, cap, body, 0, unroll=8)

    @pl.when(e % 2 == 0)
    def _():
        expert_part(ya_ref, yb_ref)

    @pl.when(e % 2 == 1)
    def _():
        expert_part(yb_ref, ya_ref)


def _cast_kernel(w_ref, o_ref):
    o_ref[...] = w_ref[...].astype(BF16).reshape(o_ref.shape)


def _split_cast_cols(w, sub):
    nl, ne, d, ff = w.shape
    return pl.pallas_call(
        _cast_kernel,
        name="expert_weight_cols",
        grid=(nl, ne, ff // sub),
        in_specs=[pl.BlockSpec((1, 1, d, sub), lambda l, e, k: (l, e, 0, k))],
        out_specs=pl.BlockSpec((1, 1, 1, d, sub), lambda l, e, k: (l, e, k, 0, 0)),
        out_shape=jax.ShapeDtypeStruct((nl, ne, ff // sub, d, sub), BF16),
        compiler_params=_cparams("parallel", "parallel", "parallel"),
    )(w)


def _split_cast_rows(w, sub):
    nl, ne, ff, d = w.shape
    rows = 4 * sub
    return pl.pallas_call(
        _cast_kernel,
        name="expert_weight_rows",
        grid=(nl, ne, ff // rows),
        in_specs=[pl.BlockSpec((1, 1, rows, d), lambda l, e, k: (l, e, k, 0))],
        out_specs=pl.BlockSpec((1, 1, rows // sub, sub, d), lambda l, e, k: (l, e, k, 0, 0)),
        out_shape=jax.ShapeDtypeStruct((nl, ne, ff // sub, sub, d), BF16),
        compiler_params=_cparams("parallel", "parallel", "parallel"),
    )(w)


def _moe(x3, idx, gates, mod, g_ffn, w_gate, w_up, w_down, *, layer, nf):
    ns, ts, d = x3.shape
    _, ne, nsub_all, _, sub = w_gate.shape
    cap = idx.shape[2]
    nsub = nsub_all // nf
    return pl.pallas_call(
        functools.partial(_moe_kernel, cap=cap, ne=ne, nf=nf),
        name="moe",
        grid=(ns, ne, nf),
        in_specs=[
            pl.BlockSpec((1, ne, 1, cap), lambda s, e, f: (s, 0, 0, 0), memory_space=pltpu.SMEM),
            pl.BlockSpec((1, ne, 1, cap), lambda s, e, f: (s, 0, 0, 0), memory_space=pltpu.SMEM),
            pl.BlockSpec((1, ts, d), lambda s, e, f: (s, 0, 0), pipeline_mode=pl.Buffered(1)),
            pl.BlockSpec((1, 8, d), lambda s, e, f: (s, 0, 0)),
            pl.BlockSpec((1, d), lambda s, e, f: (0, 0)),
            pl.BlockSpec((None, 1, nsub, d, sub), lambda s, e, f: (layer, e, f, 0, 0)),
            pl.BlockSpec((None, 1, nsub, d, sub), lambda s, e, f: (layer, e, f, 0, 0)),
            pl.BlockSpec((None, 1, nsub, sub, d), lambda s, e, f: (layer, e, f, 0, 0)),
        ],
        out_specs=pl.BlockSpec((1, ts, d), lambda s, e, f: (s, 0, 0), pipeline_mode=pl.Buffered(1)),
        out_shape=jax.ShapeDtypeStruct((ns, ts, d), F32),
        scratch_shapes=[pltpu.VMEM((cap, d), BF16), pltpu.VMEM((cap, d), F32),
                        pltpu.VMEM((cap, d), F32), pltpu.VMEM((cap, d), F32)],
        compiler_params=_cparams("parallel", "arbitrary", "arbitrary"),
    )(idx.reshape(ns, ne, 1, cap), gates.reshape(ns, ne, 1, cap), x3, mod, g_ffn, w_gate, w_up, w_down)


def _route(aff, *, n_dec, ts, batch, seq, ne):
    cap_dec = EC_CAPACITY * ts // ne
    cap_ctx = EC_CAPACITY * seq // ne
    a = aff[:, :ne]
    a_dec = a[:n_dec * ts].reshape(n_dec, ts, ne).swapaxes(1, 2)
    g_dec, i_dec = lax.top_k(a_dec, cap_dec)
    a_ctx = a[n_dec * ts:].reshape(batch, seq, ne).swapaxes(1, 2)
    g_ctx, i_ctx = lax.top_k(a_ctx, cap_ctx)
    i_ctx = i_ctx + (jnp.arange(batch) * seq)[:, None, None]
    g_ctx = g_ctx.swapaxes(0, 1).reshape(1, ne, batch * cap_ctx)
    i_ctx = i_ctx.swapaxes(0, 1).reshape(1, ne, batch * cap_ctx)
    return (jnp.concatenate([i_dec, i_ctx], axis=0).astype(jnp.int32),
            jnp.concatenate([g_dec, g_ctx], axis=0))


def _final_kernel(x_ref, g_ref, o_ref):
    x = x_ref[...]
    ms = jnp.mean(x * x, axis=-1, keepdims=True)
    o_ref[...] = (x * lax.rsqrt(ms + RMS_EPS)) * g_ref[...]


def _final_norm(x, gain, *, row0, nrows, tm):
    d = x.shape[1]
    rb0 = row0 // tm
    return pl.pallas_call(
        _final_kernel,
        name="final_norm",
        grid=(nrows // tm,),
        in_specs=[pl.BlockSpec((tm, d), lambda i: (rb0 + i, 0)),
                  pl.BlockSpec((1, d), lambda i: (0, 0))],
        out_specs=pl.BlockSpec((tm, d), lambda i: (i, 0)),
        out_shape=jax.ShapeDtypeStruct((nrows, d), F32),
        compiler_params=_cparams("parallel"),
    )(x, gain)


def kernel(x_prompt, x_sample, c, c_ctx, cache_attn_k, cache_attn_v, state_s5, state_gla, norm_mix, norm_ffn, w_mod, b_mod, w_router, ffn_w_gate, ffn_w_up, ffn_w_down, ab_w_in, ab_w_out, attn_q_norm, attn_k_norm, s5_lambda_re, s5_lambda_im, s5_log_dt, s5_b_re, s5_b_im, s5_c_re, s5_c_im, s5_d, s5_w_glu, s5_b_glu, gla_w_in, gla_w_gate_up, gla_b_gate, gla_o_norm, gla_w_out, norm_out):
    batch, seq, d = x_prompt.shape
    n_dec, ts, _ = x_sample.shape
    assert batch * seq == ts, "context batch must flatten to one latent-length sequence"
    depth = w_mod.shape[0]
    assert depth == 2 and ab_w_in.shape[0] == 1 and gla_w_in.shape[0] == 1
    ns = n_dec + 1
    ntok = ns * ts
    n_dec_tok = n_dec * ts
    ne = w_router.shape[2]
    tm = min(TOKEN_TILE, seq)
    tps = ts // tm
    qw = ATTN_HEADS * HEAD_DIM
    kw = ATTN_KV_HEADS * HEAD_DIM
    s5w = d // 2
    s5g = s5w // S5_GROUP_CH
    s5n = s5_lambda_re.shape[-1]
    past = cache_attn_k.shape[2]

    x_dec = x_sample.reshape(n_dec_tok, d)
    x_ctx = x_prompt.reshape(batch * seq, d)

    rows = -(-ns // 8) * 8
    cond = jnp.concatenate([c, c_ctx[None, :], jnp.zeros((rows - ns, d), F32)], axis=0)
    mod = _modulation(cond, w_mod, b_mod)[:, :ns].reshape(depth, ns, 6, d)
    mod = jnp.pad(mod, ((0, 0), (0, 0), (0, 2), (0, 0)))

    wr_pad = jnp.pad(w_router, ((0, 0), (0, 0), (0, LANES - ne)))
    wg_bf = _split_cast_cols(ffn_w_gate, MOE_F_SUB)
    wu_bf = _split_cast_cols(ffn_w_up, MOE_F_SUB)
    wd_bf = _split_cast_rows(ffn_w_down, MOE_F_SUB)

    def moe_layer(xin, aff, layer):
        idx, gates = _route(aff, n_dec=n_dec, ts=ts, batch=batch, seq=seq, ne=ne)
        out = _moe(xin.reshape(ns, ts, d), idx, gates, mod[layer], norm_ffn[layer:layer + 1],
                   wg_bf, wu_bf, wd_bf, layer=layer, nf=MOE_F_PARTS)
        return out.reshape(ntok, d)

    proj, u = _inproj_ab(x_dec, x_ctx, mod[0], norm_mix[0:1], ab_w_in[0].astype(BF16),
                         n_qkv=qw + 2 * kw, tps=tps, tm=tm)
    cos_tab, sin_tab = _rope_tables(ts)
    q_bf, k_bf, k_nrm, v_bf = _qkprep(proj, cos_tab, sin_tab, attn_q_norm[0], attn_k_norm[0],
                                      ts=ts, tm=tm, n_rope_seq=n_dec)
    keys = jnp.concatenate([cache_attn_k[:, 0].reshape(n_dec, past, kw).astype(BF16),
                            k_bf[:n_dec_tok].reshape(n_dec, ts, kw)], axis=1)
    vals = jnp.concatenate([cache_attn_v[:, 0].reshape(n_dec, past, kw).astype(BF16),
                            v_bf[:n_dec_tok].reshape(n_dec, ts, kw)], axis=1)
    o_attn = _attention(q_bf, keys.transpose(0, 2, 1), vals, None, row0=0, t=ts, tq=min(256, ts))
    o_attn = _attention(q_bf, k_bf[n_dec_tok:].reshape(batch, seq, kw).transpose(0, 2, 1),
                        v_bf[n_dec_tok:].reshape(batch, seq, kw), o_attn,
                        row0=n_dec_tok, t=seq, tq=min(256, seq))

    mats = _s5_matrices(s5_lambda_re[0], s5_lambda_im[0], s5_log_dt[0], s5_b_re[0], s5_b_im[0],
                        s5_c_re[0], s5_c_im[0], s5_d[0])
    h0 = state_s5[:, 0].transpose(2, 1, 4, 0, 3).reshape(s5g, 4, n_dec, s5n)
    h0 = jnp.pad(h0, ((0, 0), (0, 0), (0, 0), (0, LANES - s5n)))
    seqs = ((n_dec, ts // S5_CHUNK, 0, True), (batch, seq // S5_CHUNK, n_dec_tok // S5_CHUNK, False))
    y, hfin = _s5(u, *mats, h0, p=S5_GROUP_CH, seqs=seqs)

    w_out = ab_w_out[0].astype(BF16)
    x, aff = _ab_post(y, o_attn, x_dec, x_ctx, mod[0], s5_w_glu[0].astype(BF16), s5_b_glu[0][None, :],
                      w_out[:qw], w_out[qw:], norm_ffn[0:1], wr_pad[0], tps=tps, tm=tm, ne=ne)
    x = moe_layer(x, aff, 0)

    nh = GLA_HEADS
    dk = d // 2 // nh
    dv = d // nh
    main_w = 2 * nh * dk + 2 * nh * dv
    w_gla = gla_w_in[0]
    wl = jnp.pad(w_gla[:, main_w:], ((0, 0), (0, LANES - 2 * GLA_RANK))).astype(BF16)
    wgu = jnp.zeros((LANES, 2 * nh * dk), F32)
    wgu = wgu.at[:GLA_RANK, :nh * dk].set(gla_w_gate_up[0, 0])
    wgu = wgu.at[GLA_RANK:2 * GLA_RANK, nh * dk:].set(gla_w_gate_up[0, 1])
    bgu = jnp.concatenate([gla_b_gate[0, 0], gla_b_gate[0, 1]])[None, :]
    qk1, v1, r1, gates1 = _inproj_gla(x, mod[1], norm_mix[1:2], w_gla[:, :main_w].astype(BF16), wl, wgu, bgu,
                                      widths=(2 * nh * dk, nh * dv, nh * dv), tps=tps, tm=tm)
    s0t = state_gla[:, 0].swapaxes(-1, -2)
    o_f, o_b, _ = _gla(qk1, v1, gates1, s0t, None, row0=0, nb=n_dec, t=ts, tb=min(512, ts), dk=dk, dv=dv)
    o_f, o_b, sfin = _gla(qk1, v1, gates1, jnp.zeros((batch, 2, nh, dv, dk), F32), (o_f, o_b),
                          row0=n_dec_tok, nb=batch, t=seq, tb=min(512, seq), dk=dk, dv=dv)
    x, aff = _gla_post(o_f, o_b, r1, x, mod[1], gla_o_norm[0][None, :], gla_w_out[0].astype(BF16),
                       norm_ffn[1:2], wr_pad[1], tps=tps, tm=tm, ne=ne)
    x = moe_layer(x, aff, 1)

    gain = norm_out[None, :]
    y_sample = _final_norm(x, gain, row0=0, nrows=n_dec_tok, tm=tm).reshape(n_dec, ts, d)
    y_prompt = _final_norm(x, gain, row0=n_dec_tok, nrows=batch * seq, tm=tm).reshape(batch, seq, d)
    new_k = k_nrm[n_dec_tok:].reshape(batch, 1, seq, ATTN_KV_HEADS, HEAD_DIM)
    new_v = proj[n_dec_tok:, qw + kw:qw + 2 * kw].reshape(batch, 1, seq, ATTN_KV_HEADS, HEAD_DIM)
    new_s5 = hfin[..., :s5n].reshape(s5g, 2, 2, batch, s5n).transpose(3, 1, 0, 4, 2)[:, None]
    new_gla = sfin.swapaxes(-1, -2)[:, None]
    return (y_prompt, y_sample, new_k, new_v, new_s5, new_gla)
```

```python
import functools
import math

import jax
import jax.numpy as jnp
from jax import lax
from jax.experimental import pallas as pl
from jax.experimental.pallas import tpu as pltpu

F32 = jnp.float32
BF16 = jnp.bfloat16

RMS_EPS = 1e-6
GRID_W = 64
ROPE_THETA = 10000.0
ATTN_HEADS = 8
ATTN_KV_HEADS = 2
HEAD_DIM = 64
S5_GROUP_CH = 16
S5_CHUNK = 16
GLA_HEADS = 4
GLA_RANK = 16
GLA_GATE_NORM = 16.0
GLA_CHUNK = 64
EC_CAPACITY = 2
MOE_F_SUB = 512
MOE_F_PARTS = 2
LANES = 128
TOKEN_TILE = 256
ATTN_Q_TILE = 256
VMEM_LIMIT = 56 * 1024 * 1024


def _cparams(*sem):
    return pltpu.CompilerParams(dimension_semantics=sem, vmem_limit_bytes=VMEM_LIMIT)


def _dot(a, b):
    return jnp.dot(a, b, preferred_element_type=F32)


def _dot_nt(a, b):
    return lax.dot_general(a, b, (((1,), (1,)), ((), ())), preferred_element_type=F32)


def _split(a):
    hi = a.astype(BF16)
    lo = (a - hi.astype(F32)).astype(BF16)
    return hi, lo


def _dot3(a, b):
    ah, al = _split(a)
    bh, bl = _split(b)
    return _dot(ah, bh) + _dot(al, bh) + _dot(ah, bl)


def _dot2(a, b_exact):
    ah, al = _split(a)
    return _dot(ah, b_exact) + _dot(al, b_exact)


def _sigmoid(x):
    return 1.0 / (1.0 + jnp.exp(-x))


def _silu(x):
    return x * _sigmoid(x)


def _gelu_tanh(x):
    c = math.sqrt(2.0 / math.pi)
    return 0.5 * x * (1.0 + jnp.tanh(c * (x + 0.044715 * (x * x * x))))


def _log_sigmoid(z):
    return jnp.minimum(z, 0.0) - jnp.log1p(jnp.exp(-jnp.abs(z)))


def _rms_mod(x, gain, scale, shift):
    ms = jnp.mean(x * x, axis=-1, keepdims=True)
    return (x * lax.rsqrt(ms + RMS_EPS)) * gain * (1.0 + scale) + shift


def _router_probs(xn, mod, gain, wr, ne):
    h = _rms_mod(xn, gain, mod[4:5], mod[3:4])
    logits = _dot3(h, wr)
    lane = lax.broadcasted_iota(jnp.int32, logits.shape, 1)
    logits = jnp.where(lane < ne, logits, -jnp.inf)
    m = jnp.max(logits, axis=-1, keepdims=True)
    p = jnp.exp(logits - m)
    return p / jnp.sum(p, axis=-1, keepdims=True)


def _mod_kernel(c_ref, w_ref, b_ref, o_ref):
    c = c_ref[...]
    o_ref[0] = _dot3(_silu(c), w_ref[0]) + b_ref[0]


def _modulation(cond, w_mod, b_mod):
    depth, d, n = w_mod.shape
    r = cond.shape[0]
    tn = n // 4
    return pl.pallas_call(
        _mod_kernel,
        name="modulation",
        grid=(depth, n // tn),
        in_specs=[
            pl.BlockSpec((r, d), lambda l, j: (0, 0)),
            pl.BlockSpec((1, d, tn), lambda l, j: (l, 0, j)),
            pl.BlockSpec((1, 1, tn), lambda l, j: (l, 0, j)),
        ],
        out_specs=pl.BlockSpec((1, r, tn), lambda l, j: (l, 0, j)),
        out_shape=jax.ShapeDtypeStruct((depth, r, n), F32),
        compiler_params=_cparams("parallel", "parallel"),
    )(cond, w_mod, b_mod.reshape(depth, 1, n))


def _two_group_specs(tm, d, nd_tiles):
    return [pl.BlockSpec((tm, d), lambda i: (jnp.minimum(i, nd_tiles - 1), 0)),
            pl.BlockSpec((tm, d), lambda i: (jnp.maximum(i - nd_tiles, 0), 0))]


def _two_group_tile(xd_ref, xc_ref, nd_tiles):
    return jnp.where(pl.program_id(0) < nd_tiles, xd_ref[...], xc_ref[...])


def _to_step_major(tok_ref, out_ref):
    nq, steps, nrow, _ = out_ref.shape
    for q in range(nq):
        for i in range(steps):
            out_ref[q, i] = tok_ref[q, pl.ds(i, nrow, stride=steps), :].astype(out_ref.dtype)


def _from_step_major(in_ref, tok_ref):
    nq, steps, nrow, _ = in_ref.shape
    for q in range(nq):
        for i in range(steps):
            tok_ref[q, pl.ds(i, nrow, stride=steps), :] = in_ref[q, i].astype(F32)


def _inproj_ab_kernel(xd_ref, xc_ref, mod_ref, g_ref, w_ref, qkv_ref, u_ref, scr_ref, *, nd_tiles):
    m = mod_ref[0]
    x = _two_group_tile(xd_ref, xc_ref, nd_tiles)
    h = _rms_mod(x, g_ref[...], m[1:2], m[0:1]).astype(BF16)
    nq = qkv_ref.shape[1]
    qkv_ref[...] = _dot(h, w_ref[:, :nq])
    u = _dot(h, w_ref[:, nq:])
    for q in range(scr_ref.shape[0]):
        scr_ref[q] = u[:, q * LANES:(q + 1) * LANES]
    _to_step_major(scr_ref, u_ref)


def _inproj_ab(x_dec, x_ctx, mod, gain, w, *, n_qkv, tps, tm):
    d = x_dec.shape[1]
    nd_tiles = x_dec.shape[0] // tm
    ntok = x_dec.shape[0] + x_ctx.shape[0]
    n = w.shape[1]
    uw = n - n_qkv
    return pl.pallas_call(
        functools.partial(_inproj_ab_kernel, nd_tiles=nd_tiles),
        name="inproj_ab",
        grid=(ntok // tm,),
        in_specs=_two_group_specs(tm, d, nd_tiles) + [
            pl.BlockSpec((1, 8, d), lambda i: (i // tps, 0, 0)),
            pl.BlockSpec((1, d), lambda i: (0, 0)),
            pl.BlockSpec((d, n), lambda i: (0, 0)),
        ],
        out_specs=[pl.BlockSpec((tm, n_qkv), lambda i: (i, 0)),
                   pl.BlockSpec((uw // LANES, S5_CHUNK, tm // S5_CHUNK, LANES), lambda i: (0, 0, i, 0))],
        out_shape=[jax.ShapeDtypeStruct((ntok, n_qkv), F32),
                   jax.ShapeDtypeStruct((uw // LANES, S5_CHUNK, ntok // S5_CHUNK, LANES), BF16)],
        scratch_shapes=[pltpu.VMEM((uw // LANES, tm, LANES), F32)],
        compiler_params=_cparams("parallel"),
    )(x_dec, x_ctx, mod, gain, w)


def _inproj_gla_kernel(x_ref, mod_ref, g_ref, w_ref, wl_ref, wg_ref, bg_ref, qk_ref, v_ref, r_ref, gate_ref):
    m = mod_ref[0]
    h = _rms_mod(x_ref[...], g_ref[...], m[1:2], m[0:1]).astype(BF16)
    c0 = 0
    for ref in (qk_ref, v_ref, r_ref):
        w = ref.shape[1]
        for s in range(0, w, 512):
            ref[:, s:s + 512] = _dot(h, w_ref[:, c0 + s:c0 + s + 512]).astype(ref.dtype)
        c0 += w
    low = _dot(h, wl_ref[...])
    z = _dot3(low, wg_ref[...]) + bg_ref[...]
    gate_ref[...] = (_log_sigmoid(z) / GLA_GATE_NORM).astype(gate_ref.dtype)


def _inproj_gla(x, mod, gain, w, wl, wg, bg, *, widths, tps, tm):
    ntok, d = x.shape
    n = w.shape[1]
    ng = wg.shape[1]
    wqk, wv, wr = widths
    row = lambda width: pl.BlockSpec((tm, width), lambda i: (i, 0))
    return pl.pallas_call(
        _inproj_gla_kernel,
        name="inproj_gla",
        grid=(ntok // tm,),
        in_specs=[
            pl.BlockSpec((tm, d), lambda i: (i, 0)),
            pl.BlockSpec((1, 8, d), lambda i: (i // tps, 0, 0)),
            pl.BlockSpec((1, d), lambda i: (0, 0)),
            pl.BlockSpec((d, n), lambda i: (0, 0)),
            pl.BlockSpec((d, LANES), lambda i: (0, 0)),
            pl.BlockSpec((LANES, ng), lambda i: (0, 0)),
            pl.BlockSpec((1, ng), lambda i: (0, 0)),
        ],
        out_specs=[row(wqk), row(wv), row(wr), row(ng)],
        out_shape=[jax.ShapeDtypeStruct((ntok, wqk), BF16),
                   jax.ShapeDtypeStruct((ntok, wv), BF16),
                   jax.ShapeDtypeStruct((ntok, wr), BF16),
                   jax.ShapeDtypeStruct((ntok, ng), BF16)],
        compiler_params=_cparams("parallel"),
    )(x, mod, gain, w, wl, wg, bg)


def _qkprep_kernel(q_ref, k_ref, v_ref, cos_ref, sin_ref, qg_ref, kg_ref, oq_ref, ok_ref,
                   qo_ref, ko_ref, kn_ref, vo_ref):
    cos = cos_ref[0]
    sin = sin_ref[0]

    def prep(x, gain, ones):
        width = x.shape[1]
        ssq = _dot2(x * x, ones)
        xn = x * lax.rsqrt(ssq * (1.0 / HEAD_DIM) + RMS_EPS) * gain
        lane = lax.broadcasted_iota(jnp.int32, x.shape, 1)
        first = (lane & 16) == 0
        partner = jnp.where(first, pltpu.roll(xn, width - 16, 1), pltpu.roll(xn, 16, 1))
        rep = width // LANES
        cosw = jnp.concatenate([cos] * rep, axis=1) if rep > 1 else cos
        sinw = jnp.concatenate([sin] * rep, axis=1) if rep > 1 else sin
        return xn, xn * cosw + partner * sinw

    _, qr = prep(q_ref[...], qg_ref[...], oq_ref[...])
    qo_ref[...] = (qr * (HEAD_DIM ** -0.5)).astype(BF16)
    kn, kr = prep(k_ref[...], kg_ref[...], ok_ref[...])
    kn_ref[...] = kn
    ko_ref[...] = kr.astype(BF16)
    vo_ref[...] = v_ref[...].astype(BF16)


def _qkprep(proj, cos_tab, sin_tab, q_gain, k_gain, *, ts, tm, n_rope_seq):
    ntok = proj.shape[0]
    qw = ATTN_HEADS * HEAD_DIM
    kw = ATTN_KV_HEADS * HEAD_DIM
    tps = ts // tm

    def head_ones(width):
        a = jnp.arange(width) // HEAD_DIM
        return (a[:, None] == a[None, :]).astype(BF16)

    def tab_map(i):
        return (jnp.minimum(i // tps, n_rope_seq) // n_rope_seq, i % tps, 0)

    return pl.pallas_call(
        _qkprep_kernel,
        name="qkprep",
        grid=(ntok // tm,),
        in_specs=[
            pl.BlockSpec((tm, qw), lambda i: (i, 0)),
            pl.BlockSpec((tm, kw), lambda i: (i, qw // kw)),
            pl.BlockSpec((tm, kw), lambda i: (i, qw // kw + 1)),
            pl.BlockSpec((1, tm, LANES), tab_map),
            pl.BlockSpec((1, tm, LANES), tab_map),
            pl.BlockSpec((1, qw), lambda i: (0, 0)),
            pl.BlockSpec((1, kw), lambda i: (0, 0)),
            pl.BlockSpec((qw, qw), lambda i: (0, 0)),
            pl.BlockSpec((kw, kw), lambda i: (0, 0)),
        ],
        out_specs=[
            pl.BlockSpec((tm, qw), lambda i: (i, 0)),
            pl.BlockSpec((tm, kw), lambda i: (i, 0)),
            pl.BlockSpec((tm, kw), lambda i: (i, 0)),
            pl.BlockSpec((tm, kw), lambda i: (i, 0)),
        ],
        out_shape=[
            jax.ShapeDtypeStruct((ntok, qw), BF16),
            jax.ShapeDtypeStruct((ntok, kw), BF16),
            jax.ShapeDtypeStruct((ntok, kw), F32),
            jax.ShapeDtypeStruct((ntok, kw), BF16),
        ],
        compiler_params=_cparams("parallel"),
    )(proj, proj, proj, cos_tab, sin_tab,
      jnp.tile(q_gain, ATTN_HEADS).reshape(1, qw), jnp.tile(k_gain, ATTN_KV_HEADS).reshape(1, kw),
      head_ones(qw), head_ones(kw))


def _rope_tables(t):
    nfreq = HEAD_DIM // 4
    inv = ROPE_THETA ** (-jnp.arange(nfreq, dtype=F32) / nfreq)
    pos = jnp.arange(t)
    row = (pos // GRID_W).astype(F32)[:, None] * inv[None, :]
    col = (pos % GRID_W).astype(F32)[:, None] * inv[None, :]
    cos = jnp.concatenate([jnp.cos(row), jnp.cos(row), jnp.cos(col), jnp.cos(col)], axis=1)
    sin = jnp.concatenate([-jnp.sin(row), jnp.sin(row), -jnp.sin(col), jnp.sin(col)], axis=1)
    rep = LANES // HEAD_DIM
    cos = jnp.tile(cos, (1, rep))
    sin = jnp.tile(sin, (1, rep))
    return (jnp.stack([cos, jnp.ones_like(cos)]), jnp.stack([sin, jnp.zeros_like(sin)]))


def _attn_kernel(q_ref, kt_ref, v_ref, *rest):
    o_ref = rest[-1]
    q = q_ref[...]
    v = v_ref[0]
    group = ATTN_HEADS // ATTN_KV_HEADS
    outs = []
    for h in range(ATTN_HEADS):
        kv = h // group
        qh = q[:, h * HEAD_DIM:(h + 1) * HEAD_DIM]
        kt = kt_ref[0, kv * HEAD_DIM:(kv + 1) * HEAD_DIM, :]
        s = _dot(qh, kt)
        p = jnp.exp(s - jnp.max(s, axis=-1, keepdims=True))
        l = jnp.sum(p, axis=-1, keepdims=True)
        o = _dot(p.astype(BF16), v)
        outs.append(o[:, kv * HEAD_DIM:(kv + 1) * HEAD_DIM] / l)
    o_ref[...] = jnp.concatenate(outs, axis=1).astype(BF16)


def _attention(q, keys_t, vals, prev, *, row0, t, tq):
    b, kw, s = keys_t.shape
    ntok, qw = q.shape
    nq = t // tq
    rb0 = row0 // tq
    extra = [] if prev is None else [prev]
    return pl.pallas_call(
        _attn_kernel,
        name="attention",
        grid=(b, nq),
        in_specs=[
            pl.BlockSpec((tq, qw), lambda bi, i: (rb0 + bi * nq + i, 0)),
            pl.BlockSpec((1, kw, s), lambda bi, i: (bi, 0, 0)),
            pl.BlockSpec((1, s, kw), lambda bi, i: (bi, 0, 0)),
        ] + [pl.BlockSpec(memory_space=pl.ANY)] * len(extra),
        out_specs=pl.BlockSpec((tq, qw), lambda bi, i: (rb0 + bi * nq + i, 0)),
        out_shape=jax.ShapeDtypeStruct((ntok, qw), BF16),
        input_output_aliases={3: 0} if extra else {},
        compiler_params=_cparams("parallel", "parallel"),
    )(q, keys_t, vals, *extra)


def _s5_kernel(u_ref, e_ref, m_ref, f_ref, lam_ref, h0_ref, y_ref, hfin_ref, xg_ref, s_ref, hin_ref, *, p, seqs):
    steps, nr = u_ref.shape[1], u_ref.shape[2]
    gpb = LANES // p
    lane_blk = lax.broadcasted_iota(jnp.int32, (1, LANES), 1) // p
    y_ref[...] = jnp.zeros(y_ref.shape, y_ref.dtype)

    def group(g8, carry):
        halves = []
        for half in range(steps // gpb):
            acc = jnp.zeros((nr, LANES), F32)
            for j in range(gpb):
                slab = u_ref[0, half * gpb + j].astype(F32)
                rolled = pltpu.roll(slab, (p * j + LANES - p * g8) & (LANES - 1), 1)
                acc = jnp.where(lane_blk == j, rolled, acc)
            halves.append(acc)
        xg_ref[...] = jnp.concatenate(halves, axis=1).astype(BF16)
        xg = xg_ref[...]
        eh, el = _split(e_ref[g8])
        ends = _dot(xg, eh) + _dot(xg, el)
        for k in range(4):
            s_ref[k] = ends[:, k * LANES:(k + 1) * LANES]
        lam = lam_ref[g8]

        for nb, nc, row0, carries in seqs:
            lfr = jnp.broadcast_to(lam[0:1], (nb, LANES))
            lfi = jnp.broadcast_to(lam[1:2], (nb, LANES))
            lbr = jnp.broadcast_to(lam[2:3], (nb, LANES))
            lbi = jnp.broadcast_to(lam[3:4], (nb, LANES))

            def step(c, carry, nb=nb, nc=nc, row0=row0, lfr=lfr, lfi=lfi, lbr=lbr, lbi=lbi):
                hfr, hfi, hbr, hbi = carry
                rf = pl.ds(row0 + c, nb, stride=nc)
                rb = pl.ds(row0 + nc - 1 - c, nb, stride=nc)
                hin_ref[0, rf, :] = hfr
                hin_ref[1, rf, :] = hfi
                hin_ref[2, rb, :] = hbr
                hin_ref[3, rb, :] = hbi
                sfr = s_ref[0, rf, :]
                sfi = s_ref[1, rf, :]
                sbr = s_ref[2, rb, :]
                sbi = s_ref[3, rb, :]
                return (lfr * hfr - lfi * hfi + sfr, lfr * hfi + lfi * hfr + sfi,
                        lbr * hbr - lbi * hbi + sbr, lbr * hbi + lbi * hbr + sbi)

            if carries:
                h0 = h0_ref[g8]
                init = (h0[0], h0[1], h0[2], h0[3])
            else:
                init = (jnp.zeros((nb, LANES), F32),) * 4
            fin = lax.fori_loop(0, nc, step, init)
            if not carries:
                for k in range(4):
                    hfin_ref[g8, k] = fin[k]

        mh, ml = _split(m_ref[g8])
        y = _dot(xg, mh) + _dot(xg, ml)
        for k in range(0, 4, 2):
            hin = jnp.concatenate([hin_ref[k], hin_ref[k + 1]], axis=1)
            y = y + _dot3(hin, f_ref[g8, k * LANES:(k + 2) * LANES, :])
        for half in range(steps // gpb):
            piece = y[:, half * LANES:(half + 1) * LANES]
            for j in range(gpb):
                rolled = pltpu.roll(piece, (p * g8 + LANES - p * j) & (LANES - 1), 1)
                i = half * gpb + j
                y_ref[0, i] = jnp.where(lane_blk == g8, rolled.astype(y_ref.dtype), y_ref[0, i])
        return carry

    lax.fori_loop(0, gpb, group, 0)


def _s5(u5, e_mat, m_mat, f_mat, lam_l, h0, *, p, seqs):
    nq, steps, nr, _ = u5.shape
    gpb = LANES // p
    w = steps * p
    sw = 4 * LANES
    nb_carry = h0.shape[2]
    nb_free = [s[0] for s in seqs if not s[3]][0]
    once = pl.Buffered(1)
    return pl.pallas_call(
        functools.partial(_s5_kernel, p=p, seqs=seqs),
        name="s5",
        grid=(nq,),
        in_specs=[
            pl.BlockSpec((1, steps, nr, LANES), lambda q: (q, 0, 0, 0), pipeline_mode=once),
            pl.BlockSpec((gpb, w, sw), lambda q: (q, 0, 0), pipeline_mode=once),
            pl.BlockSpec((gpb, w, w), lambda q: (q, 0, 0), pipeline_mode=once),
            pl.BlockSpec((gpb, sw, w), lambda q: (q, 0, 0), pipeline_mode=once),
            pl.BlockSpec((gpb, 8, LANES), lambda q: (q, 0, 0)),
            pl.BlockSpec((gpb, 4, nb_carry, LANES), lambda q: (q, 0, 0, 0)),
        ],
        out_specs=[pl.BlockSpec((1, steps, nr, LANES), lambda q: (q, 0, 0, 0), pipeline_mode=once),
                   pl.BlockSpec((gpb, 4, nb_free, LANES), lambda q: (q, 0, 0, 0))],
        out_shape=[jax.ShapeDtypeStruct(u5.shape, BF16),
                   jax.ShapeDtypeStruct((nq * gpb, 4, nb_free, LANES), F32)],
        scratch_shapes=[pltpu.VMEM((nr, w), BF16), pltpu.VMEM((4, nr, LANES), F32), pltpu.VMEM((4, nr, LANES), F32)],
        compiler_params=_cparams("parallel"),
    )(u5, e_mat, m_mat, f_mat, lam_l, h0)


def _s5_matrices(lam_re, lam_im, log_dt, b_re, b_im, c_re, c_im, d_skip):
    hi = lax.Precision.HIGHEST
    L = S5_CHUNK
    _, g, n = lam_re.shape
    p = b_re.shape[-1]
    dt = jnp.exp(log_dt)[..., None]
    mag = jnp.exp(lam_re * dt)
    lbr, lbi = mag * jnp.cos(lam_im * dt), mag * jnp.sin(lam_im * dt)
    den = lam_re * lam_re + lam_im * lam_im
    cr = ((lbr - 1.0) * lam_re + lbi * lam_im) / den
    ci = (lbi * lam_re - (lbr - 1.0) * lam_im) / den
    bbr = cr[..., None] * b_re - ci[..., None] * b_im
    bbi = cr[..., None] * b_im + ci[..., None] * b_re
    pr, pi = [jnp.ones_like(lbr)], [jnp.zeros_like(lbr)]
    for _ in range(L):
        pr, pi = pr + [pr[-1] * lbr - pi[-1] * lbi], pi + [pr[-1] * lbi + pi[-1] * lbr]
    pwr, pwi = jnp.stack(pr, axis=2), jnp.stack(pi, axis=2)
    cpr = c_re[:, :, None] * pwr[:, :, :, None, :] - c_im[:, :, None] * pwi[:, :, :, None, :]
    cpi = c_re[:, :, None] * pwi[:, :, :, None, :] + c_im[:, :, None] * pwr[:, :, :, None, :]
    kern = (jnp.einsum('dgkpn,dgnq->dgkpq', cpr[:, :, :L], bbr, precision=hi)
            - jnp.einsum('dgkpn,dgnq->dgkpq', cpi[:, :, :L], bbi, precision=hi))
    step = jnp.arange(L)
    lag_f = step[None, :] - step[:, None]
    lag_b = -lag_f

    def toeplitz(kd, lag):
        t = jnp.take(kd, jnp.clip(lag, 0, L - 1), axis=1)
        t = jnp.where((lag >= 0)[None, :, :, None, None], t, 0.0)
        return t.transpose(0, 1, 4, 2, 3).reshape(g, L * p, L * p)

    eye_l = jnp.eye(L, dtype=F32)
    dmat = jnp.einsum('ji,gq,qp->gjqip', eye_l, d_skip, jnp.eye(p, dtype=F32)).reshape(g, L * p, L * p)
    m_mat = toeplitz(kern[0], lag_f) + toeplitz(kern[1], lag_b) + dmat

    def e_block(pw_r, pw_i, b_r, b_i):
        er = pw_r[:, :, None, :] * b_r.transpose(0, 2, 1)[:, None] - pw_i[:, :, None, :] * b_i.transpose(0, 2, 1)[:, None]
        ei = pw_r[:, :, None, :] * b_i.transpose(0, 2, 1)[:, None] + pw_i[:, :, None, :] * b_r.transpose(0, 2, 1)[:, None]
        return er.reshape(g, L * p, n), ei.reshape(g, L * p, n)

    def pad_cols(a):
        return jnp.pad(a, ((0, 0), (0, 0), (0, LANES - n)))

    efr, efi = e_block(pwr[0][:, L - 1 - step], pwi[0][:, L - 1 - step], bbr[0], bbi[0])
    ebr, ebi = e_block(pwr[1][:, step], pwi[1][:, step], bbr[1], bbi[1])
    e_mat = jnp.concatenate([pad_cols(efr), pad_cols(efi), pad_cols(ebr), pad_cols(ebi)], axis=2)

    def f_block(c_r, c_i):
        fr = c_r.transpose(0, 3, 1, 2).reshape(g, n, L * p)
        fi = -c_i.transpose(0, 3, 1, 2).reshape(g, n, L * p)
        return fr, fi

    def pad_rows(a):
        return jnp.pad(a, ((0, 0), (0, LANES - n), (0, 0)))

    ffr, ffi = f_block(cpr[0][:, step + 1], cpi[0][:, step + 1])
    fbr, fbi = f_block(cpr[1][:, L - step], cpi[1][:, L - step])
    f_mat = jnp.concatenate([pad_rows(ffr), pad_rows(ffi), pad_rows(fbr), pad_rows(fbi)], axis=1)

    lam_l = jnp.stack([pwr[0, :, L], pwi[0, :, L], pwr[1, :, L], pwi[1, :, L]], axis=1)
    lam_l = jnp.pad(lam_l, ((0, 0), (0, 4), (0, LANES - n)))
    return e_mat, m_mat, f_mat, lam_l


def _ab_post_kernel(y_ref, oa_ref, xd_ref, xc_ref, mod_ref, wglu_ref, bglu_ref, woa_ref, wos_ref, gffn_ref,
                    wr_ref, xo_ref, aff_ref, ytok_ref, *, ne, nd_tiles):
    m = mod_ref[0]
    half = wos_ref.shape[0]
    _from_step_major(y_ref, ytok_ref)
    y = jnp.concatenate([ytok_ref[q] for q in range(ytok_ref.shape[0])], axis=1)
    glu = _dot(_gelu_tanh(y).astype(BF16), wglu_ref[...]) + bglu_ref[...]
    o_s5 = glu[:, :half] * _sigmoid(glu[:, half:])
    out = _dot(oa_ref[...], woa_ref[...]) + _dot(o_s5.astype(BF16), wos_ref[...])
    xn = _two_group_tile(xd_ref, xc_ref, nd_tiles) + m[2:3] * out
    xo_ref[...] = xn
    aff_ref[...] = _router_probs(xn, m, gffn_ref[...], wr_ref[...], ne)


def _ab_post(y, o_attn, x_dec, x_ctx, mod, w_glu, b_glu, w_out_a, w_out_s, g_ffn, w_router, *, tps, tm, ne):
    d = x_dec.shape[1]
    nd_tiles = x_dec.shape[0] // tm
    ntok = x_dec.shape[0] + x_ctx.shape[0]
    nq, steps, _, _ = y.shape
    full = lambda shape: pl.BlockSpec(shape, lambda i: (0,) * len(shape))
    return pl.pallas_call(
        functools.partial(_ab_post_kernel, ne=ne, nd_tiles=nd_tiles),
        name="ab_post",
        grid=(ntok // tm,),
        scratch_shapes=[pltpu.VMEM((nq, tm, LANES), F32)],
        in_specs=[
            pl.BlockSpec((nq, steps, tm // steps, LANES), lambda i: (0, 0, i, 0)),
            pl.BlockSpec((tm, o_attn.shape[1]), lambda i: (i, 0)),
        ] + _two_group_specs(tm, d, nd_tiles) + [
            pl.BlockSpec((1, 8, d), lambda i: (i // tps, 0, 0)),
            full(w_glu.shape), full(b_glu.shape), full(w_out_a.shape), full(w_out_s.shape),
            full(g_ffn.shape), full(w_router.shape),
        ],
        out_specs=[pl.BlockSpec((tm, d), lambda i: (i, 0)),
                   pl.BlockSpec((tm, LANES), lambda i: (i, 0))],
        out_shape=[jax.ShapeDtypeStruct((ntok, d), F32),
                   jax.ShapeDtypeStruct((ntok, LANES), F32)],
        compiler_params=_cparams("parallel"),
    )(y, o_attn, x_dec, x_ctx, mod, w_glu, b_glu, w_out_a, w_out_s, g_ffn, w_router)


def _gla_post_kernel(of_ref, ob_ref, r_ref, x_ref, mod_ref, gn_ref, wo_ref, gffn_ref, wr_ref, xo_ref, aff_ref,
                     *, ne):
    m = mod_ref[0]
    o = of_ref[...].astype(F32) + ob_ref[...].astype(F32)
    dv = gn_ref.shape[1]
    parts = []
    for h in range(GLA_HEADS):
        oh = o[:, h * dv:(h + 1) * dv]
        ms = jnp.mean(oh * oh, axis=-1, keepdims=True)
        parts.append(oh * lax.rsqrt(ms + RMS_EPS) * gn_ref[...])
    on = jnp.concatenate(parts, axis=1) * _silu(r_ref[...].astype(F32))
    out = _dot(on.astype(BF16), wo_ref[...])
    xn = x_ref[...] + m[2:3] * out
    xo_ref[...] = xn
    aff_ref[...] = _router_probs(xn, m, gffn_ref[...], wr_ref[...], ne)


def _gla_post(o_fwd, o_bwd, r, x, mod, g_norm, w_out, g_ffn, w_router, *, tps, tm, ne):
    ntok, d = x.shape
    vw = o_fwd.shape[1]
    full = lambda shape: pl.BlockSpec(shape, lambda i: (0,) * len(shape))
    return pl.pallas_call(
        functools.partial(_gla_post_kernel, ne=ne),
        name="gla_post",
        grid=(ntok // tm,),
        in_specs=[
            pl.BlockSpec((tm, vw), lambda i: (i, 0)),
            pl.BlockSpec((tm, vw), lambda i: (i, 0)),
            pl.BlockSpec((tm, vw), lambda i: (i, 0)),
            pl.BlockSpec((tm, d), lambda i: (i, 0)),
            pl.BlockSpec((1, 8, d), lambda i: (i // tps, 0, 0)),
            full(g_norm.shape), full(w_out.shape), full(g_ffn.shape), full(w_router.shape),
        ],
        out_specs=[pl.BlockSpec((tm, d), lambda i: (i, 0)),
                   pl.BlockSpec((tm, LANES), lambda i: (i, 0))],
        out_shape=[jax.ShapeDtypeStruct((ntok, d), F32),
                   jax.ShapeDtypeStruct((ntok, LANES), F32)],
        compiler_params=_cparams("parallel"),
    )(o_fwd, o_bwd, r, x, mod, g_norm, w_out, g_ffn, w_router)


def _gla_kernel(qf_ref, kf_ref, vf_ref, gf_ref, qb_ref, kb_ref, vb_ref, gb_ref, s0_ref, *rest,
                nchunk, dk, dv, nh):
    of_ref, ob_ref, sfin_ref, st_ref, qd_ref, ke_ref, tot_ref, oacc_ref = rest[-8:]
    j = pl.program_id(1)
    c = GLA_CHUNK
    tb = nchunk * c

    @pl.when(j == 0)
    def _():
        st_ref[...] = s0_ref[0]

    row = lax.broadcasted_iota(jnp.int32, (tb, tb), 0)
    col = lax.broadcasted_iota(jnp.int32, (tb, tb), 1)
    same = (row // c) == (col // c)
    same_b = same.astype(BF16)
    dirs = ((qf_ref, kf_ref, vf_ref, gf_ref, of_ref, same & (row >= col)),
            (qb_ref, kb_ref, vb_ref, gb_ref, ob_ref, same & (row <= col)))
    for d, (q_ref, k_ref, v_ref, g_ref, o_ref, tri) in enumerate(dirs):
        g = g_ref[...]
        cum = _dot(tri.astype(BF16), g)
        tot = _dot(same_b, g)
        k = k_ref[...].astype(F32)
        q_dec = (q_ref[...].astype(F32) * (dk ** -0.5) * jnp.exp(cum)).astype(BF16)
        k_inv = (k * jnp.exp(-cum)).astype(BF16)
        qd_ref[d] = q_dec
        ke_ref[d] = (k * jnp.exp(tot - cum)).astype(BF16)
        tot_ref[d] = tot
        for h in range(nh):
            ks = slice(h * dk, (h + 1) * dk)
            vs = slice(h * dv, (h + 1) * dv)
            scores = jnp.where(tri, _dot_nt(q_dec[:, ks], k_inv[:, ks]), 0.0)
            oacc_ref[d, :, vs] = _dot(scores.astype(BF16), v_ref[:, vs])

    def step(i, carry):
        for d, (_, _, v_ref, _, _, _) in enumerate(dirs):
            r0 = pl.multiple_of((i if d == 0 else nchunk - 1 - i) * c, c)
            decay = jnp.exp(tot_ref[d, pl.ds(r0, 1), :])
            for h in range(nh):
                ks = slice(h * dk, (h + 1) * dk)
                vs = slice(h * dv, (h + 1) * dv)
                st = st_ref[d, h]
                oacc_ref[d, pl.ds(r0, c), vs] += _dot_nt(qd_ref[d, pl.ds(r0, c), ks], st.astype(BF16))
                vt = v_ref[pl.ds(r0, c), vs].astype(F32).T.astype(BF16)
                st_ref[d, h] = st * decay[:, ks] + _dot(vt, ke_ref[d, pl.ds(r0, c), ks])
        return carry

    lax.fori_loop(0, nchunk, step, 0)
    of_ref[...] = oacc_ref[0].astype(of_ref.dtype)
    ob_ref[...] = oacc_ref[1].astype(ob_ref.dtype)

    @pl.when(j == pl.num_programs(1) - 1)
    def _():
        sfin_ref[0] = st_ref[...]


def _gla(qk, v, gates, s0t, prev, *, row0, nb, t, tb, dk, dv):
    h = GLA_HEADS
    ntok = qk.shape[0]
    nblk = t // tb
    rb0 = row0 // tb
    kw, vw = h * dk, h * dv
    fwd = lambda b, j: rb0 + b * nblk + j
    bwd = lambda b, j: rb0 + b * nblk + nblk - 1 - j

    def views(rows, gate_col):
        return [pl.BlockSpec((tb, kw), lambda b, j: (rows(b, j), 0)),
                pl.BlockSpec((tb, kw), lambda b, j: (rows(b, j), 1)),
                pl.BlockSpec((tb, vw), lambda b, j: (rows(b, j), 0)),
                pl.BlockSpec((tb, kw), lambda b, j: (rows(b, j), gate_col))]

    extra = [] if prev is None else list(prev)
    state_spec = pl.BlockSpec((1, 2, h, dv, dk), lambda b, j: (b, 0, 0, 0, 0))
    return pl.pallas_call(
        functools.partial(_gla_kernel, nchunk=tb // GLA_CHUNK, dk=dk, dv=dv, nh=h),
        name="gla",
        input_output_aliases={9: 0, 10: 1} if extra else {},
        grid=(nb, nblk),
        in_specs=views(fwd, 0) + views(bwd, 1) + [state_spec] + [pl.BlockSpec(memory_space=pl.ANY)] * len(extra),
        out_specs=[
            pl.BlockSpec((tb, vw), lambda b, j: (fwd(b, j), 0)),
            pl.BlockSpec((tb, vw), lambda b, j: (bwd(b, j), 0)),
            state_spec,
        ],
        out_shape=[jax.ShapeDtypeStruct((ntok, vw), BF16),
                   jax.ShapeDtypeStruct((ntok, vw), BF16),
                   jax.ShapeDtypeStruct((nb, 2, h, dv, dk), F32)],
        scratch_shapes=[pltpu.VMEM((2, h, dv, dk), F32), pltpu.VMEM((2, tb, kw), BF16),
                        pltpu.VMEM((2, tb, kw), BF16), pltpu.VMEM((2, tb, kw), F32),
                        pltpu.VMEM((2, tb, vw), F32)],
        compiler_params=_cparams("parallel", "arbitrary"),
    )(qk, qk, v, gates, qk, qk, v, gates, s0t, *extra)


def _moe_kernel(idx_ref, gsm_ref, x_ref, mod_ref, gffn_ref, wg_ref, wu_ref, wd_ref, o_ref,
                xs_ref, gbuf_ref, ya_ref, yb_ref, *, cap, ne, nf):
    e = pl.program_id(1)
    f = pl.program_id(2)
    m = mod_ref[0]
    g2 = m[5:6]
    nsub = wg_ref.shape[1]
    rows_per = cap // (nf * nsub)
    e_next = jnp.minimum(e + 1, ne - 1)
    e_prev = jnp.maximum(e - 1, 0)

    def gather_row(en, r):
        gbuf_ref[pl.ds(r, 1), :] = x_ref[0, pl.ds(idx_ref[0, en, 0, r], 1), :]

    def scatter_row(ep, y_ref, r):
        i = idx_ref[0, ep, 0, r]
        o_ref[0, pl.ds(i, 1), :] += y_ref[pl.ds(r, 1), :] * gsm_ref[0, ep, 0, r] * g2

    @pl.when((e == 0) & (f == 0))
    def _():
        o_ref[...] = x_ref[...]
        yb_ref[...] = jnp.zeros(yb_ref.shape, F32)

        def body(r, carry):
            gather_row(0, r)
            return carry
        lax.fori_loop(0, cap, body, 0, unroll=8)

    @pl.when(f == 0)
    def _():
        xs_ref[...] = _rms_mod(gbuf_ref[...], gffn_ref[...], m[4:5], m[3:4]).astype(BF16)

    def expert_part(y_cur, y_prev):
        @pl.when(f == 0)
        def _():
            y_cur[...] = jnp.zeros(y_cur.shape, F32)

        def sub_chunk(k, carry):
            r0 = pl.multiple_of((f * nsub + k) * rows_per, rows_per)
            for r in range(rows_per):
                gather_row(e_next, r0 + r)
                scatter_row(e_prev, y_prev, r0 + r)
            xs = xs_ref[...]
            hid = _silu(_dot(xs, wg_ref[0, k])) * _dot(xs, wu_ref[0, k])
            y_cur[...] += _dot(hid.astype(BF16), wd_ref[0, k])
            return carry

        lax.fori_loop(0, nsub, sub_chunk, 0)

        @pl.when((e == ne - 1) & (f == nf - 1))
        def _():
            def body(r, carry):
                scatter_row(e, y_cur, r)
                return carry
            lax.fori_loop(0, cap, body, 0, unroll=8)

    @pl.when(e % 2 == 0)
    def _():
        expert_part(ya_ref, yb_ref)

    @pl.when(e % 2 == 1)
    def _():
        expert_part(yb_ref, ya_ref)


def _cast_kernel(w_ref, o_ref):
    o_ref[...] = w_ref[...].astype(BF16).reshape(o_ref.shape)


def _cast_cols_kernel(w_ref, o_ref):
    sub = o_ref.shape[-1]
    for k in range(o_ref.shape[2]):
        o_ref[0, 0, k] = w_ref[0, 0, :, k * sub:(k + 1) * sub].astype(BF16)


def _split_cast_cols(w, sub):
    nl, ne, d, ff = w.shape
    per = 1024 // sub
    return pl.pallas_call(
        _cast_cols_kernel,
        name="expert_weight_cols",
        grid=(nl, ne, ff // (per * sub)),
        in_specs=[pl.BlockSpec((1, 1, d, per * sub), lambda l, e, k: (l, e, 0, k))],
        out_specs=pl.BlockSpec((1, 1, per, d, sub), lambda l, e, k: (l, e, k, 0, 0)),
        out_shape=jax.ShapeDtypeStruct((nl, ne, ff // sub, d, sub), BF16),
        compiler_params=_cparams("parallel", "parallel", "parallel"),
    )(w)


def _split_cast_rows(w, sub):
    nl, ne, ff, d = w.shape
    rows = max(sub, 1024)
    return pl.pallas_call(
        _cast_kernel,
        name="expert_weight_rows",
        grid=(nl, ne, ff // rows),
        in_specs=[pl.BlockSpec((1, 1, rows, d), lambda l, e, k: (l, e, k, 0))],
        out_specs=pl.BlockSpec((1, 1, rows // sub, sub, d), lambda l, e, k: (l, e, k, 0, 0)),
        out_shape=jax.ShapeDtypeStruct((nl, ne, ff // sub, sub, d), BF16),
        compiler_params=_cparams("parallel", "parallel", "parallel"),
    )(w)


def _moe(x3, idx, gates, mod, g_ffn, w_gate, w_up, w_down, *, layer, nf):
    ns, ts, d = x3.shape
    _, ne, nsub_all, _, sub = w_gate.shape
    cap = idx.shape[2]
    nsub = nsub_all // nf
    return pl.pallas_call(
        functools.partial(_moe_kernel, cap=cap, ne=ne, nf=nf),
        name="moe",
        grid=(ns, ne, nf),
        in_specs=[
            pl.BlockSpec((1, ne, 1, cap), lambda s, e, f: (s, 0, 0, 0), memory_space=pltpu.SMEM),
            pl.BlockSpec((1, ne, 1, cap), lambda s, e, f: (s, 0, 0, 0), memory_space=pltpu.SMEM),
            pl.BlockSpec((1, ts, d), lambda s, e, f: (s, 0, 0), pipeline_mode=pl.Buffered(1)),
            pl.BlockSpec((1, 8, d), lambda s, e, f: (s, 0, 0)),
            pl.BlockSpec((1, d), lambda s, e, f: (0, 0)),
            pl.BlockSpec((None, 1, nsub, d, sub), lambda s, e, f: (layer, e, f, 0, 0)),
            pl.BlockSpec((None, 1, nsub, d, sub), lambda s, e, f: (layer, e, f, 0, 0)),
            pl.BlockSpec((None, 1, nsub, sub, d), lambda s, e, f: (layer, e, f, 0, 0)),
        ],
        out_specs=pl.BlockSpec((1, ts, d), lambda s, e, f: (s, 0, 0), pipeline_mode=pl.Buffered(1)),
        out_shape=jax.ShapeDtypeStruct((ns, ts, d), F32),
        scratch_shapes=[pltpu.VMEM((cap, d), BF16), pltpu.VMEM((cap, d), F32),
                        pltpu.VMEM((cap, d), F32), pltpu.VMEM((cap, d), F32)],
        compiler_params=_cparams("parallel", "arbitrary", "arbitrary"),
    )(idx.reshape(ns, ne, 1, cap), gates.reshape(ns, ne, 1, cap), x3, mod, g_ffn, w_gate, w_up, w_down)


def _route(aff, *, n_dec, ts, batch, seq, ne):
    cap_dec = EC_CAPACITY * ts // ne
    cap_ctx = EC_CAPACITY * seq // ne
    a = aff[:, :ne]
    a_dec = a[:n_dec * ts].reshape(n_dec, ts, ne).swapaxes(1, 2)
    g_dec, i_dec = lax.top_k(a_dec, cap_dec)
    a_ctx = a[n_dec * ts:].reshape(batch, seq, ne).swapaxes(1, 2)
    g_ctx, i_ctx = lax.top_k(a_ctx, cap_ctx)
    i_ctx = i_ctx + (jnp.arange(batch) * seq)[:, None, None]
    g_ctx = g_ctx.swapaxes(0, 1).reshape(1, ne, batch * cap_ctx)
    i_ctx = i_ctx.swapaxes(0, 1).reshape(1, ne, batch * cap_ctx)
    return (jnp.concatenate([i_dec, i_ctx], axis=0).astype(jnp.int32),
            jnp.concatenate([g_dec, g_ctx], axis=0))


def _final_kernel(x_ref, g_ref, o_ref):
    x = x_ref[...]
    ms = jnp.mean(x * x, axis=-1, keepdims=True)
    o_ref[...] = (x * lax.rsqrt(ms + RMS_EPS)) * g_ref[...]


def _final_norm(x, gain, *, row0, nrows, tm):
    d = x.shape[1]
    rb0 = row0 // tm
    return pl.pallas_call(
        _final_kernel,
        name="final_norm",
        grid=(nrows // tm,),
        in_specs=[pl.BlockSpec((tm, d), lambda i: (rb0 + i, 0)),
                  pl.BlockSpec((1, d), lambda i: (0, 0))],
        out_specs=pl.BlockSpec((tm, d), lambda i: (i, 0)),
        out_shape=jax.ShapeDtypeStruct((nrows, d), F32),
        compiler_params=_cparams("parallel"),
    )(x, gain)


def kernel(x_prompt, x_sample, c, c_ctx, cache_attn_k, cache_attn_v, state_s5, state_gla, norm_mix, norm_ffn, w_mod, b_mod, w_router, ffn_w_gate, ffn_w_up, ffn_w_down, ab_w_in, ab_w_out, attn_q_norm, attn_k_norm, s5_lambda_re, s5_lambda_im, s5_log_dt, s5_b_re, s5_b_im, s5_c_re, s5_c_im, s5_d, s5_w_glu, s5_b_glu, gla_w_in, gla_w_gate_up, gla_b_gate, gla_o_norm, gla_w_out, norm_out):
    batch, seq, d = x_prompt.shape
    n_dec, ts, _ = x_sample.shape
    assert batch * seq == ts, "context batch must flatten to one latent-length sequence"
    depth = w_mod.shape[0]
    assert depth == 2 and ab_w_in.shape[0] == 1 and gla_w_in.shape[0] == 1
    ns = n_dec + 1
    ntok = ns * ts
    n_dec_tok = n_dec * ts
    ne = w_router.shape[2]
    tm = min(TOKEN_TILE, ts)
    tps = ts // tm
    qw = ATTN_HEADS * HEAD_DIM
    kw = ATTN_KV_HEADS * HEAD_DIM
    s5w = d // 2
    s5g = s5w // S5_GROUP_CH
    s5n = s5_lambda_re.shape[-1]
    past = cache_attn_k.shape[2]

    x_dec = x_sample.reshape(n_dec_tok, d)
    x_ctx = x_prompt.reshape(batch * seq, d)

    rows = -(-ns // 8) * 8
    cond = jnp.concatenate([c, c_ctx[None, :], jnp.zeros((rows - ns, d), F32)], axis=0)
    mod = _modulation(cond, w_mod, b_mod)[:, :ns].reshape(depth, ns, 6, d)
    mod = jnp.pad(mod, ((0, 0), (0, 0), (0, 2), (0, 0)))

    wr_pad = jnp.pad(w_router, ((0, 0), (0, 0), (0, LANES - ne)))
    wg_bf = _split_cast_cols(ffn_w_gate, MOE_F_SUB)
    wu_bf = _split_cast_cols(ffn_w_up, MOE_F_SUB)
    wd_bf = _split_cast_rows(ffn_w_down, MOE_F_SUB)

    def moe_layer(xin, aff, layer):
        idx, gates = _route(aff, n_dec=n_dec, ts=ts, batch=batch, seq=seq, ne=ne)
        out = _moe(xin.reshape(ns, ts, d), idx, gates, mod[layer], norm_ffn[layer:layer + 1],
                   wg_bf, wu_bf, wd_bf, layer=layer, nf=MOE_F_PARTS)
        return out.reshape(ntok, d)

    proj, u = _inproj_ab(x_dec, x_ctx, mod[0], norm_mix[0:1], ab_w_in[0].astype(BF16),
                         n_qkv=qw + 2 * kw, tps=tps, tm=tm)
    cos_tab, sin_tab = _rope_tables(ts)
    q_bf, k_bf, k_nrm, v_bf = _qkprep(proj, cos_tab, sin_tab, attn_q_norm[0], attn_k_norm[0],
                                      ts=ts, tm=tm, n_rope_seq=n_dec)
    keys = jnp.concatenate([cache_attn_k[:, 0].reshape(n_dec, past, kw).astype(BF16),
                            k_bf[:n_dec_tok].reshape(n_dec, ts, kw)], axis=1)
    vals = jnp.concatenate([cache_attn_v[:, 0].reshape(n_dec, past, kw).astype(BF16),
                            v_bf[:n_dec_tok].reshape(n_dec, ts, kw)], axis=1)
    o_attn = _attention(q_bf, keys.transpose(0, 2, 1), vals, None, row0=0, t=ts, tq=min(ATTN_Q_TILE, ts))
    o_attn = _attention(q_bf, k_bf[n_dec_tok:].reshape(batch, seq, kw).transpose(0, 2, 1),
                        v_bf[n_dec_tok:].reshape(batch, seq, kw), o_attn,
                        row0=n_dec_tok, t=seq, tq=min(ATTN_Q_TILE, seq))

    mats = _s5_matrices(s5_lambda_re[0], s5_lambda_im[0], s5_log_dt[0], s5_b_re[0], s5_b_im[0],
                        s5_c_re[0], s5_c_im[0], s5_d[0])
    h0 = state_s5[:, 0].transpose(2, 1, 4, 0, 3).reshape(s5g, 4, n_dec, s5n)
    h0 = jnp.pad(h0, ((0, 0), (0, 0), (0, 0), (0, LANES - s5n)))
    seqs = ((n_dec, ts // S5_CHUNK, 0, True), (batch, seq // S5_CHUNK, n_dec_tok // S5_CHUNK, False))
    y, hfin = _s5(u, *mats, h0, p=S5_GROUP_CH, seqs=seqs)

    w_out = ab_w_out[0].astype(BF16)
    x, aff = _ab_post(y, o_attn, x_dec, x_ctx, mod[0], s5_w_glu[0].astype(BF16), s5_b_glu[0][None, :],
                      w_out[:qw], w_out[qw:], norm_ffn[0:1], wr_pad[0], tps=tps, tm=tm, ne=ne)
    x = moe_layer(x, aff, 0)

    nh = GLA_HEADS
    dk = d // 2 // nh
    dv = d // nh
    main_w = 2 * nh * dk + 2 * nh * dv
    w_gla = gla_w_in[0]
    wl = jnp.pad(w_gla[:, main_w:], ((0, 0), (0, LANES - 2 * GLA_RANK))).astype(BF16)
    wgu = jnp.zeros((LANES, 2 * nh * dk), F32)
    wgu = wgu.at[:GLA_RANK, :nh * dk].set(gla_w_gate_up[0, 0])
    wgu = wgu.at[GLA_RANK:2 * GLA_RANK, nh * dk:].set(gla_w_gate_up[0, 1])
    bgu = jnp.concatenate([gla_b_gate[0, 0], gla_b_gate[0, 1]])[None, :]
    qk1, v1, r1, gates1 = _inproj_gla(x, mod[1], norm_mix[1:2], w_gla[:, :main_w].astype(BF16), wl, wgu, bgu,
                                      widths=(2 * nh * dk, nh * dv, nh * dv), tps=tps, tm=tm)
    s0t = state_gla[:, 0].swapaxes(-1, -2)
    o_f, o_b, _ = _gla(qk1, v1, gates1, s0t, None, row0=0, nb=n_dec, t=ts, tb=min(512, ts), dk=dk, dv=dv)
    o_f, o_b, sfin = _gla(qk1, v1, gates1, jnp.zeros((batch, 2, nh, dv, dk), F32), (o_f, o_b),
                          row0=n_dec_tok, nb=batch, t=seq, tb=min(512, seq), dk=dk, dv=dv)
    x, aff = _gla_post(o_f, o_b, r1, x, mod[1], gla_o_norm[0][None, :], gla_w_out[0].astype(BF16),
                       norm_ffn[1:2], wr_pad[1], tps=tps, tm=tm, ne=ne)
    x = moe_layer(x, aff, 1)

    gain = norm_out[None, :]
    tn = min(4 * tm, ts)
    y_sample = _final_norm(x, gain, row0=0, nrows=n_dec_tok, tm=tn).reshape(n_dec, ts, d)
    y_prompt = _final_norm(x, gain, row0=n_dec_tok, nrows=batch * seq, tm=tn).reshape(batch, seq, d)
    new_k = k_nrm[n_dec_tok:].reshape(batch, 1, seq, ATTN_KV_HEADS, HEAD_DIM)
    new_v = proj[n_dec_tok:, qw + kw:qw + 2 * kw].reshape(batch, 1, seq, ATTN_KV_HEADS, HEAD_DIM)
    new_s5 = hfin[..., :s5n].reshape(s5g, 2, 2, batch, s5n).transpose(3, 1, 0, 4, 2)[:, None]
    new_gla = sfin.swapaxes(-1, -2)[:, None]
    return (y_prompt, y_sample, new_k, new_v, new_s5, new_gla)
```

```python
import functools
import math

import jax
import jax.numpy as jnp
from jax import lax
from jax.experimental import pallas as pl
from jax.experimental.pallas import tpu as pltpu

F32 = jnp.float32
BF16 = jnp.bfloat16

RMS_EPS = 1e-6
GRID_W = 64
ROPE_THETA = 10000.0
ATTN_HEADS = 8
ATTN_KV_HEADS = 2
HEAD_DIM = 64
S5_GROUP_CH = 16
S5_CHUNK = 16
GLA_HEADS = 4
GLA_RANK = 16
GLA_GATE_NORM = 16.0
GLA_CHUNK = 64
GLA_BLOCK = 256
EC_CAPACITY = 2
MOE_F_SUB = 512
MOE_F_PARTS = 2
LANES = 128
TOKEN_TILE = 256
ATTN_Q_TILE = 256
VMEM_LIMIT = 56 * 1024 * 1024


def _cparams(*sem):
    return pltpu.CompilerParams(dimension_semantics=sem, vmem_limit_bytes=VMEM_LIMIT)


def _dot(a, b):
    return jnp.dot(a, b, preferred_element_type=F32)


def _dot_nt(a, b):
    return lax.dot_general(a, b, (((1,), (1,)), ((), ())), preferred_element_type=F32)


def _split(a):
    hi = a.astype(BF16)
    lo = (a - hi.astype(F32)).astype(BF16)
    return hi, lo


def _dot3(a, b):
    ah, al = _split(a)
    bh, bl = _split(b)
    return _dot(ah, bh) + _dot(al, bh) + _dot(ah, bl)


def _dot2(a, b_exact):
    ah, al = _split(a)
    return _dot(ah, b_exact) + _dot(al, b_exact)


def _sigmoid(x):
    return 1.0 / (1.0 + jnp.exp(-x))


def _silu(x):
    return x * _sigmoid(x)


def _gelu_tanh(x):
    c = math.sqrt(2.0 / math.pi)
    return 0.5 * x * (1.0 + jnp.tanh(c * (x + 0.044715 * (x * x * x))))


def _log_sigmoid(z):
    return jnp.minimum(z, 0.0) - jnp.log1p(jnp.exp(-jnp.abs(z)))


def _rms_mod(x, gain, scale, shift):
    ms = jnp.mean(x * x, axis=-1, keepdims=True)
    return (x * lax.rsqrt(ms + RMS_EPS)) * gain * (1.0 + scale) + shift


def _router_probs(xn, mod, gain, wr, ne):
    h = _rms_mod(xn, gain, mod[4:5], mod[3:4])
    logits = _dot3(h, wr)
    lane = lax.broadcasted_iota(jnp.int32, logits.shape, 1)
    logits = jnp.where(lane < ne, logits, -jnp.inf)
    m = jnp.max(logits, axis=-1, keepdims=True)
    p = jnp.exp(logits - m)
    return p / jnp.sum(p, axis=-1, keepdims=True)


def _mod_kernel(c_ref, w_ref, b_ref, o_ref):
    c = c_ref[...]
    o_ref[0] = _dot3(_silu(c), w_ref[0]) + b_ref[0]


def _modulation(cond, w_mod, b_mod):
    depth, d, n = w_mod.shape
    r = cond.shape[0]
    tn = n // 4
    return pl.pallas_call(
        _mod_kernel,
        name="modulation",
        grid=(depth, n // tn),
        in_specs=[
            pl.BlockSpec((r, d), lambda l, j: (0, 0)),
            pl.BlockSpec((1, d, tn), lambda l, j: (l, 0, j)),
            pl.BlockSpec((1, 1, tn), lambda l, j: (l, 0, j)),
        ],
        out_specs=pl.BlockSpec((1, r, tn), lambda l, j: (l, 0, j)),
        out_shape=jax.ShapeDtypeStruct((depth, r, n), F32),
        compiler_params=_cparams("parallel", "parallel"),
    )(cond, w_mod, b_mod.reshape(depth, 1, n))


def _two_group_specs(tm, d, nd_tiles):
    return [pl.BlockSpec((tm, d), lambda i: (jnp.minimum(i, nd_tiles - 1), 0)),
            pl.BlockSpec((tm, d), lambda i: (jnp.maximum(i - nd_tiles, 0), 0))]


def _two_group_tile(xd_ref, xc_ref, nd_tiles):
    return jnp.where(pl.program_id(0) < nd_tiles, xd_ref[...], xc_ref[...])


def _to_step_major(tok_ref, out_ref):
    nq, steps, nrow, _ = out_ref.shape
    for q in range(nq):
        for i in range(steps):
            out_ref[q, i] = tok_ref[q, pl.ds(i, nrow, stride=steps), :].astype(out_ref.dtype)


def _from_step_major(in_ref, tok_ref):
    nq, steps, nrow, _ = in_ref.shape
    for q in range(nq):
        for i in range(steps):
            tok_ref[q, pl.ds(i, nrow, stride=steps), :] = in_ref[q, i].astype(F32)


def _inproj_ab_kernel(xd_ref, xc_ref, mod_ref, g_ref, w_ref, qkv_ref, u_ref, scr_ref, *, nd_tiles):
    m = mod_ref[0]
    x = _two_group_tile(xd_ref, xc_ref, nd_tiles)
    h = _rms_mod(x, g_ref[...], m[1:2], m[0:1]).astype(BF16)
    nq = qkv_ref.shape[1]
    qkv_ref[...] = _dot(h, w_ref[:, :nq])
    u = _dot(h, w_ref[:, nq:])
    for q in range(scr_ref.shape[0]):
        scr_ref[q] = u[:, q * LANES:(q + 1) * LANES]
    _to_step_major(scr_ref, u_ref)


def _inproj_ab(x_dec, x_ctx, mod, gain, w, *, n_qkv, tps, tm):
    d = x_dec.shape[1]
    nd_tiles = x_dec.shape[0] // tm
    ntok = x_dec.shape[0] + x_ctx.shape[0]
    n = w.shape[1]
    uw = n - n_qkv
    return pl.pallas_call(
        functools.partial(_inproj_ab_kernel, nd_tiles=nd_tiles),
        name="inproj_ab",
        grid=(ntok // tm,),
        in_specs=_two_group_specs(tm, d, nd_tiles) + [
            pl.BlockSpec((1, 8, d), lambda i: (i // tps, 0, 0)),
            pl.BlockSpec((1, d), lambda i: (0, 0)),
            pl.BlockSpec((d, n), lambda i: (0, 0)),
        ],
        out_specs=[pl.BlockSpec((tm, n_qkv), lambda i: (i, 0)),
                   pl.BlockSpec((uw // LANES, S5_CHUNK, tm // S5_CHUNK, LANES), lambda i: (0, 0, i, 0))],
        out_shape=[jax.ShapeDtypeStruct((ntok, n_qkv), F32),
                   jax.ShapeDtypeStruct((uw // LANES, S5_CHUNK, ntok // S5_CHUNK, LANES), BF16)],
        scratch_shapes=[pltpu.VMEM((uw // LANES, tm, LANES), F32)],
        compiler_params=_cparams("parallel"),
    )(x_dec, x_ctx, mod, gain, w)


def _inproj_gla_kernel(x_ref, mod_ref, g_ref, w_ref, wl_ref, wg_ref, bg_ref, qk_ref, v_ref, r_ref, gate_ref):
    m = mod_ref[0]
    h = _rms_mod(x_ref[...], g_ref[...], m[1:2], m[0:1]).astype(BF16)
    c0 = 0
    for ref in (qk_ref, v_ref, r_ref):
        w = ref.shape[1]
        for s in range(0, w, 512):
            ref[:, s:s + 512] = _dot(h, w_ref[:, c0 + s:c0 + s + 512]).astype(ref.dtype)
        c0 += w
    low = _dot(h, wl_ref[...])
    z = _dot3(low, wg_ref[...]) + bg_ref[...]
    gate_ref[...] = (_log_sigmoid(z) / GLA_GATE_NORM).astype(gate_ref.dtype)


def _inproj_gla(x, mod, gain, w, wl, wg, bg, *, widths, tps, tm):
    ntok, d = x.shape
    n = w.shape[1]
    ng = wg.shape[1]
    wqk, wv, wr = widths
    row = lambda width: pl.BlockSpec((tm, width), lambda i: (i, 0))
    return pl.pallas_call(
        _inproj_gla_kernel,
        name="inproj_gla",
        grid=(ntok // tm,),
        in_specs=[
            pl.BlockSpec((tm, d), lambda i: (i, 0)),
            pl.BlockSpec((1, 8, d), lambda i: (i // tps, 0, 0)),
            pl.BlockSpec((1, d), lambda i: (0, 0)),
            pl.BlockSpec((d, n), lambda i: (0, 0)),
            pl.BlockSpec((d, LANES), lambda i: (0, 0)),
            pl.BlockSpec((LANES, ng), lambda i: (0, 0)),
            pl.BlockSpec((1, ng), lambda i: (0, 0)),
        ],
        out_specs=[row(wqk), row(wv), row(wr), row(ng)],
        out_shape=[jax.ShapeDtypeStruct((ntok, wqk), BF16),
                   jax.ShapeDtypeStruct((ntok, wv), BF16),
                   jax.ShapeDtypeStruct((ntok, wr), BF16),
                   jax.ShapeDtypeStruct((ntok, ng), BF16)],
        compiler_params=_cparams("parallel"),
    )(x, mod, gain, w, wl, wg, bg)


def _qkprep_kernel(q_ref, k_ref, v_ref, cos_ref, sin_ref, qg_ref, kg_ref, oq_ref, ok_ref,
                   qo_ref, ko_ref, kn_ref, vo_ref):
    cos = cos_ref[0]
    sin = sin_ref[0]

    def prep(x, gain, ones):
        width = x.shape[1]
        ssq = _dot2(x * x, ones)
        xn = x * lax.rsqrt(ssq * (1.0 / HEAD_DIM) + RMS_EPS) * gain
        lane = lax.broadcasted_iota(jnp.int32, x.shape, 1)
        first = (lane & 16) == 0
        partner = jnp.where(first, pltpu.roll(xn, width - 16, 1), pltpu.roll(xn, 16, 1))
        rep = width // LANES
        cosw = jnp.concatenate([cos] * rep, axis=1) if rep > 1 else cos
        sinw = jnp.concatenate([sin] * rep, axis=1) if rep > 1 else sin
        return xn, xn * cosw + partner * sinw

    _, qr = prep(q_ref[...], qg_ref[...], oq_ref[...])
    qo_ref[...] = (qr * (HEAD_DIM ** -0.5)).astype(BF16)
    kn, kr = prep(k_ref[...], kg_ref[...], ok_ref[...])
    kn_ref[...] = kn
    ko_ref[...] = kr.astype(BF16)
    vo_ref[...] = v_ref[...].astype(BF16)


def _qkprep(proj, cos_tab, sin_tab, q_gain, k_gain, *, ts, tm, n_rope_seq):
    ntok = proj.shape[0]
    qw = ATTN_HEADS * HEAD_DIM
    kw = ATTN_KV_HEADS * HEAD_DIM
    tps = ts // tm

    def head_ones(width):
        a = jnp.arange(width) // HEAD_DIM
        return (a[:, None] == a[None, :]).astype(BF16)

    def tab_map(i):
        return (jnp.minimum(i // tps, n_rope_seq) // n_rope_seq, i % tps, 0)

    return pl.pallas_call(
        _qkprep_kernel,
        name="qkprep",
        grid=(ntok // tm,),
        in_specs=[
            pl.BlockSpec((tm, qw), lambda i: (i, 0)),
            pl.BlockSpec((tm, kw), lambda i: (i, qw // kw)),
            pl.BlockSpec((tm, kw), lambda i: (i, qw // kw + 1)),
            pl.BlockSpec((1, tm, LANES), tab_map),
            pl.BlockSpec((1, tm, LANES), tab_map),
            pl.BlockSpec((1, qw), lambda i: (0, 0)),
            pl.BlockSpec((1, kw), lambda i: (0, 0)),
            pl.BlockSpec((qw, qw), lambda i: (0, 0)),
            pl.BlockSpec((kw, kw), lambda i: (0, 0)),
        ],
        out_specs=[
            pl.BlockSpec((tm, qw), lambda i: (i, 0)),
            pl.BlockSpec((tm, kw), lambda i: (i, 0)),
            pl.BlockSpec((tm, kw), lambda i: (i, 0)),
            pl.BlockSpec((tm, kw), lambda i: (i, 0)),
        ],
        out_shape=[
            jax.ShapeDtypeStruct((ntok, qw), BF16),
            jax.ShapeDtypeStruct((ntok, kw), BF16),
            jax.ShapeDtypeStruct((ntok, kw), F32),
            jax.ShapeDtypeStruct((ntok, kw), BF16),
        ],
        compiler_params=_cparams("parallel"),
    )(proj, proj, proj, cos_tab, sin_tab,
      jnp.tile(q_gain, ATTN_HEADS).reshape(1, qw), jnp.tile(k_gain, ATTN_KV_HEADS).reshape(1, kw),
      head_ones(qw), head_ones(kw))


def _rope_tables(t):
    nfreq = HEAD_DIM // 4
    inv = ROPE_THETA ** (-jnp.arange(nfreq, dtype=F32) / nfreq)
    pos = jnp.arange(t)
    row = (pos // GRID_W).astype(F32)[:, None] * inv[None, :]
    col = (pos % GRID_W).astype(F32)[:, None] * inv[None, :]
    cos = jnp.concatenate([jnp.cos(row), jnp.cos(row), jnp.cos(col), jnp.cos(col)], axis=1)
    sin = jnp.concatenate([-jnp.sin(row), jnp.sin(row), -jnp.sin(col), jnp.sin(col)], axis=1)
    rep = LANES // HEAD_DIM
    cos = jnp.tile(cos, (1, rep))
    sin = jnp.tile(sin, (1, rep))
    return (jnp.stack([cos, jnp.ones_like(cos)]), jnp.stack([sin, jnp.zeros_like(sin)]))


def _attn_kernel(q_ref, kt_ref, v_ref, *rest):
    o_ref = rest[-1]
    q = q_ref[...]
    v = v_ref[0]
    group = ATTN_HEADS // ATTN_KV_HEADS
    outs = []
    for h in range(ATTN_HEADS):
        kv = h // group
        qh = q[:, h * HEAD_DIM:(h + 1) * HEAD_DIM]
        kt = kt_ref[0, kv * HEAD_DIM:(kv + 1) * HEAD_DIM, :]
        s = _dot(qh, kt)
        p = jnp.exp(s - jnp.max(s, axis=-1, keepdims=True))
        l = jnp.sum(p, axis=-1, keepdims=True)
        o = _dot(p.astype(BF16), v)
        outs.append(o[:, kv * HEAD_DIM:(kv + 1) * HEAD_DIM] / l)
    o_ref[...] = jnp.concatenate(outs, axis=1).astype(BF16)


def _attention(q, keys_t, vals, prev, *, row0, t, tq):
    b, kw, s = keys_t.shape
    ntok, qw = q.shape
    nq = t // tq
    rb0 = row0 // tq
    extra = [] if prev is None else [prev]
    return pl.pallas_call(
        _attn_kernel,
        name="attention",
        grid=(b, nq),
        in_specs=[
            pl.BlockSpec((tq, qw), lambda bi, i: (rb0 + bi * nq + i, 0)),
            pl.BlockSpec((1, kw, s), lambda bi, i: (bi, 0, 0)),
            pl.BlockSpec((1, s, kw), lambda bi, i: (bi, 0, 0)),
        ] + [pl.BlockSpec(memory_space=pl.ANY)] * len(extra),
        out_specs=pl.BlockSpec((tq, qw), lambda bi, i: (rb0 + bi * nq + i, 0)),
        out_shape=jax.ShapeDtypeStruct((ntok, qw), BF16),
        input_output_aliases={3: 0} if extra else {},
        compiler_params=_cparams("parallel", "parallel"),
    )(q, keys_t, vals, *extra)


def _s5_kernel(u_ref, e_ref, m_ref, f_ref, lam_ref, h0_ref, y_ref, hfin_ref, xg_ref, s_ref, hin_ref, *, p, seqs):
    steps, nr = u_ref.shape[1], u_ref.shape[2]
    gpb = LANES // p
    lane_blk = lax.broadcasted_iota(jnp.int32, (1, LANES), 1) // p
    y_ref[...] = jnp.zeros(y_ref.shape, y_ref.dtype)

    def group(g8, carry):
        halves = []
        for half in range(steps // gpb):
            acc = jnp.zeros((nr, LANES), F32)
            for j in range(gpb):
                slab = u_ref[0, half * gpb + j].astype(F32)
                rolled = pltpu.roll(slab, (p * j + LANES - p * g8) & (LANES - 1), 1)
                acc = jnp.where(lane_blk == j, rolled, acc)
            halves.append(acc)
        xg_ref[...] = jnp.concatenate(halves, axis=1).astype(BF16)
        xg = xg_ref[...]
        eh, el = _split(e_ref[g8])
        ends = _dot(xg, eh) + _dot(xg, el)
        for k in range(4):
            s_ref[k] = ends[:, k * LANES:(k + 1) * LANES]
        lam = lam_ref[g8]

        for nb, nc, row0, carries in seqs:
            lfr = jnp.broadcast_to(lam[0:1], (nb, LANES))
            lfi = jnp.broadcast_to(lam[1:2], (nb, LANES))
            lbr = jnp.broadcast_to(lam[2:3], (nb, LANES))
            lbi = jnp.broadcast_to(lam[3:4], (nb, LANES))

            def step(c, carry, nb=nb, nc=nc, row0=row0, lfr=lfr, lfi=lfi, lbr=lbr, lbi=lbi):
                hfr, hfi, hbr, hbi = carry
                rf = pl.ds(row0 + c, nb, stride=nc)
                rb = pl.ds(row0 + nc - 1 - c, nb, stride=nc)
                hin_ref[0, rf, :] = hfr
                hin_ref[1, rf, :] = hfi
                hin_ref[2, rb, :] = hbr
                hin_ref[3, rb, :] = hbi
                sfr = s_ref[0, rf, :]
                sfi = s_ref[1, rf, :]
                sbr = s_ref[2, rb, :]
                sbi = s_ref[3, rb, :]
                return (lfr * hfr - lfi * hfi + sfr, lfr * hfi + lfi * hfr + sfi,
                        lbr * hbr - lbi * hbi + sbr, lbr * hbi + lbi * hbr + sbi)

            if carries:
                h0 = h0_ref[g8]
                init = (h0[0], h0[1], h0[2], h0[3])
            else:
                init = (jnp.zeros((nb, LANES), F32),) * 4
            fin = lax.fori_loop(0, nc, step, init, unroll=4)
            if not carries:
                for k in range(4):
                    hfin_ref[g8, k] = fin[k]

        mh, ml = _split(m_ref[g8])
        y = _dot(xg, mh) + _dot(xg, ml)
        for k in range(0, 4, 2):
            hin = jnp.concatenate([hin_ref[k], hin_ref[k + 1]], axis=1)
            y = y + _dot3(hin, f_ref[g8, k * LANES:(k + 2) * LANES, :])
        for half in range(steps // gpb):
            piece = y[:, half * LANES:(half + 1) * LANES]
            for j in range(gpb):
                rolled = pltpu.roll(piece, (p * g8 + LANES - p * j) & (LANES - 1), 1)
                i = half * gpb + j
                y_ref[0, i] = jnp.where(lane_blk == g8, rolled.astype(y_ref.dtype), y_ref[0, i])
        return carry

    lax.fori_loop(0, gpb, group, 0)


def _s5(u5, e_mat, m_mat, f_mat, lam_l, h0, *, p, seqs):
    nq, steps, nr, _ = u5.shape
    gpb = LANES // p
    w = steps * p
    sw = 4 * LANES
    nb_carry = h0.shape[2]
    nb_free = [s[0] for s in seqs if not s[3]][0]
    once = pl.Buffered(1)
    return pl.pallas_call(
        functools.partial(_s5_kernel, p=p, seqs=seqs),
        name="s5",
        grid=(nq,),
        in_specs=[
            pl.BlockSpec((1, steps, nr, LANES), lambda q: (q, 0, 0, 0), pipeline_mode=once),
            pl.BlockSpec((gpb, w, sw), lambda q: (q, 0, 0), pipeline_mode=once),
            pl.BlockSpec((gpb, w, w), lambda q: (q, 0, 0), pipeline_mode=once),
            pl.BlockSpec((gpb, sw, w), lambda q: (q, 0, 0), pipeline_mode=once),
            pl.BlockSpec((gpb, 8, LANES), lambda q: (q, 0, 0)),
            pl.BlockSpec((gpb, 4, nb_carry, LANES), lambda q: (q, 0, 0, 0)),
        ],
        out_specs=[pl.BlockSpec((1, steps, nr, LANES), lambda q: (q, 0, 0, 0), pipeline_mode=once),
                   pl.BlockSpec((gpb, 4, nb_free, LANES), lambda q: (q, 0, 0, 0))],
        out_shape=[jax.ShapeDtypeStruct(u5.shape, BF16),
                   jax.ShapeDtypeStruct((nq * gpb, 4, nb_free, LANES), F32)],
        scratch_shapes=[pltpu.VMEM((nr, w), BF16), pltpu.VMEM((4, nr, LANES), F32), pltpu.VMEM((4, nr, LANES), F32)],
        compiler_params=_cparams("parallel"),
    )(u5, e_mat, m_mat, f_mat, lam_l, h0)


def _s5_matrices(lam_re, lam_im, log_dt, b_re, b_im, c_re, c_im, d_skip):
    hi = lax.Precision.HIGHEST
    L = S5_CHUNK
    _, g, n = lam_re.shape
    p = b_re.shape[-1]
    dt = jnp.exp(log_dt)[..., None]
    mag = jnp.exp(lam_re * dt)
    lbr, lbi = mag * jnp.cos(lam_im * dt), mag * jnp.sin(lam_im * dt)
    den = lam_re * lam_re + lam_im * lam_im
    cr = ((lbr - 1.0) * lam_re + lbi * lam_im) / den
    ci = (lbi * lam_re - (lbr - 1.0) * lam_im) / den
    bbr = cr[..., None] * b_re - ci[..., None] * b_im
    bbi = cr[..., None] * b_im + ci[..., None] * b_re
    pr, pi = [jnp.ones_like(lbr)], [jnp.zeros_like(lbr)]
    for _ in range(L):
        pr, pi = pr + [pr[-1] * lbr - pi[-1] * lbi], pi + [pr[-1] * lbi + pi[-1] * lbr]
    pwr, pwi = jnp.stack(pr, axis=2), jnp.stack(pi, axis=2)
    cpr = c_re[:, :, None] * pwr[:, :, :, None, :] - c_im[:, :, None] * pwi[:, :, :, None, :]
    cpi = c_re[:, :, None] * pwi[:, :, :, None, :] + c_im[:, :, None] * pwr[:, :, :, None, :]
    kern = (jnp.einsum('dgkpn,dgnq->dgkpq', cpr[:, :, :L], bbr, precision=hi)
            - jnp.einsum('dgkpn,dgnq->dgkpq', cpi[:, :, :L], bbi, precision=hi))
    step = jnp.arange(L)
    lag_f = step[None, :] - step[:, None]
    lag_b = -lag_f

    def toeplitz(kd, lag):
        t = jnp.take(kd, jnp.clip(lag, 0, L - 1), axis=1)
        t = jnp.where((lag >= 0)[None, :, :, None, None], t, 0.0)
        return t.transpose(0, 1, 4, 2, 3).reshape(g, L * p, L * p)

    eye_l = jnp.eye(L, dtype=F32)
    dmat = jnp.einsum('ji,gq,qp->gjqip', eye_l, d_skip, jnp.eye(p, dtype=F32)).reshape(g, L * p, L * p)
    m_mat = toeplitz(kern[0], lag_f) + toeplitz(kern[1], lag_b) + dmat

    def e_block(pw_r, pw_i, b_r, b_i):
        er = pw_r[:, :, None, :] * b_r.transpose(0, 2, 1)[:, None] - pw_i[:, :, None, :] * b_i.transpose(0, 2, 1)[:, None]
        ei = pw_r[:, :, None, :] * b_i.transpose(0, 2, 1)[:, None] + pw_i[:, :, None, :] * b_r.transpose(0, 2, 1)[:, None]
        return er.reshape(g, L * p, n), ei.reshape(g, L * p, n)

    def pad_cols(a):
        return jnp.pad(a, ((0, 0), (0, 0), (0, LANES - n)))

    efr, efi = e_block(pwr[0][:, L - 1 - step], pwi[0][:, L - 1 - step], bbr[0], bbi[0])
    ebr, ebi = e_block(pwr[1][:, step], pwi[1][:, step], bbr[1], bbi[1])
    e_mat = jnp.concatenate([pad_cols(efr), pad_cols(efi), pad_cols(ebr), pad_cols(ebi)], axis=2)

    def f_block(c_r, c_i):
        fr = c_r.transpose(0, 3, 1, 2).reshape(g, n, L * p)
        fi = -c_i.transpose(0, 3, 1, 2).reshape(g, n, L * p)
        return fr, fi

    def pad_rows(a):
        return jnp.pad(a, ((0, 0), (0, LANES - n), (0, 0)))

    ffr, ffi = f_block(cpr[0][:, step + 1], cpi[0][:, step + 1])
    fbr, fbi = f_block(cpr[1][:, L - step], cpi[1][:, L - step])
    f_mat = jnp.concatenate([pad_rows(ffr), pad_rows(ffi), pad_rows(fbr), pad_rows(fbi)], axis=1)

    lam_l = jnp.stack([pwr[0, :, L], pwi[0, :, L], pwr[1, :, L], pwi[1, :, L]], axis=1)
    lam_l = jnp.pad(lam_l, ((0, 0), (0, 4), (0, LANES - n)))
    return e_mat, m_mat, f_mat, lam_l


def _ab_post_kernel(y_ref, oa_ref, xd_ref, xc_ref, mod_ref, wglu_ref, bglu_ref, woa_ref, wos_ref, gffn_ref,
                    wr_ref, xo_ref, aff_ref, ytok_ref, *, ne, nd_tiles):
    m = mod_ref[0]
    half = wos_ref.shape[0]
    _from_step_major(y_ref, ytok_ref)
    y = jnp.concatenate([ytok_ref[q] for q in range(ytok_ref.shape[0])], axis=1)
    glu = _dot(_gelu_tanh(y).astype(BF16), wglu_ref[...]) + bglu_ref[...]
    o_s5 = glu[:, :half] * _sigmoid(glu[:, half:])
    out = _dot(oa_ref[...], woa_ref[...]) + _dot(o_s5.astype(BF16), wos_ref[...])
    xn = _two_group_tile(xd_ref, xc_ref, nd_tiles) + m[2:3] * out
    xo_ref[...] = xn
    aff_ref[...] = _router_probs(xn, m, gffn_ref[...], wr_ref[...], ne)


def _ab_post(y, o_attn, x_dec, x_ctx, mod, w_glu, b_glu, w_out_a, w_out_s, g_ffn, w_router, *, tps, tm, ne):
    d = x_dec.shape[1]
    nd_tiles = x_dec.shape[0] // tm
    ntok = x_dec.shape[0] + x_ctx.shape[0]
    nq, steps, _, _ = y.shape
    full = lambda shape: pl.BlockSpec(shape, lambda i: (0,) * len(shape))
    return pl.pallas_call(
        functools.partial(_ab_post_kernel, ne=ne, nd_tiles=nd_tiles),
        name="ab_post",
        grid=(ntok // tm,),
        scratch_shapes=[pltpu.VMEM((nq, tm, LANES), F32)],
        in_specs=[
            pl.BlockSpec((nq, steps, tm // steps, LANES), lambda i: (0, 0, i, 0)),
            pl.BlockSpec((tm, o_attn.shape[1]), lambda i: (i, 0)),
        ] + _two_group_specs(tm, d, nd_tiles) + [
            pl.BlockSpec((1, 8, d), lambda i: (i // tps, 0, 0)),
            full(w_glu.shape), full(b_glu.shape), full(w_out_a.shape), full(w_out_s.shape),
            full(g_ffn.shape), full(w_router.shape),
        ],
        out_specs=[pl.BlockSpec((tm, d), lambda i: (i, 0)),
                   pl.BlockSpec((tm, LANES), lambda i: (i, 0))],
        out_shape=[jax.ShapeDtypeStruct((ntok, d), F32),
                   jax.ShapeDtypeStruct((ntok, LANES), F32)],
        compiler_params=_cparams("parallel"),
    )(y, o_attn, x_dec, x_ctx, mod, w_glu, b_glu, w_out_a, w_out_s, g_ffn, w_router)


def _gla_post_kernel(of_ref, ob_ref, r_ref, x_ref, mod_ref, gn_ref, wo_ref, gffn_ref, wr_ref, xo_ref, aff_ref,
                     *, ne):
    m = mod_ref[0]
    o = of_ref[...].astype(F32) + ob_ref[...].astype(F32)
    dv = gn_ref.shape[1]
    parts = []
    for h in range(GLA_HEADS):
        oh = o[:, h * dv:(h + 1) * dv]
        ms = jnp.mean(oh * oh, axis=-1, keepdims=True)
        parts.append(oh * lax.rsqrt(ms + RMS_EPS) * gn_ref[...])
    on = jnp.concatenate(parts, axis=1) * _silu(r_ref[...].astype(F32))
    out = _dot(on.astype(BF16), wo_ref[...])
    xn = x_ref[...] + m[2:3] * out
    xo_ref[...] = xn
    aff_ref[...] = _router_probs(xn, m, gffn_ref[...], wr_ref[...], ne)


def _gla_post(o_fwd, o_bwd, r, x, mod, g_norm, w_out, g_ffn, w_router, *, tps, tm, ne):
    ntok, d = x.shape
    vw = o_fwd.shape[1]
    full = lambda shape: pl.BlockSpec(shape, lambda i: (0,) * len(shape))
    return pl.pallas_call(
        functools.partial(_gla_post_kernel, ne=ne),
        name="gla_post",
        grid=(ntok // tm,),
        in_specs=[
            pl.BlockSpec((tm, vw), lambda i: (i, 0)),
            pl.BlockSpec((tm, vw), lambda i: (i, 0)),
            pl.BlockSpec((tm, vw), lambda i: (i, 0)),
            pl.BlockSpec((tm, d), lambda i: (i, 0)),
            pl.BlockSpec((1, 8, d), lambda i: (i // tps, 0, 0)),
            full(g_norm.shape), full(w_out.shape), full(g_ffn.shape), full(w_router.shape),
        ],
        out_specs=[pl.BlockSpec((tm, d), lambda i: (i, 0)),
                   pl.BlockSpec((tm, LANES), lambda i: (i, 0))],
        out_shape=[jax.ShapeDtypeStruct((ntok, d), F32),
                   jax.ShapeDtypeStruct((ntok, LANES), F32)],
        compiler_params=_cparams("parallel"),
    )(o_fwd, o_bwd, r, x, mod, g_norm, w_out, g_ffn, w_router)


def _gla_kernel(qf_ref, kf_ref, vf_ref, gf_ref, qb_ref, kb_ref, vb_ref, gb_ref, s0_ref, *rest,
                nchunk, dk, dv, nh):
    of_ref, ob_ref, sfin_ref, st_ref, qd_ref, ke_ref, tot_ref, oacc_ref = rest[-8:]
    j = pl.program_id(1)
    c = GLA_CHUNK
    tb = nchunk * c

    @pl.when(j == 0)
    def _():
        st_ref[...] = s0_ref[0]

    row = lax.broadcasted_iota(jnp.int32, (tb, tb), 0)
    col = lax.broadcasted_iota(jnp.int32, (tb, tb), 1)
    same = (row // c) == (col // c)
    same_b = same.astype(BF16)
    dirs = ((qf_ref, kf_ref, vf_ref, gf_ref, of_ref, same & (row >= col)),
            (qb_ref, kb_ref, vb_ref, gb_ref, ob_ref, same & (row <= col)))
    for d, (q_ref, k_ref, v_ref, g_ref, o_ref, tri) in enumerate(dirs):
        g = g_ref[...]
        cum = _dot(tri.astype(BF16), g)
        tot = _dot(same_b, g)
        k = k_ref[...].astype(F32)
        q_dec = (q_ref[...].astype(F32) * (dk ** -0.5) * jnp.exp(cum)).astype(BF16)
        k_inv = (k * jnp.exp(-cum)).astype(BF16)
        qd_ref[d] = q_dec
        ke_ref[d] = (k * jnp.exp(tot - cum)).astype(BF16)
        tot_ref[d] = tot
        for h in range(nh):
            ks = slice(h * dk, (h + 1) * dk)
            vs = slice(h * dv, (h + 1) * dv)
            scores = jnp.where(tri, _dot_nt(q_dec[:, ks], k_inv[:, ks]), 0.0)
            oacc_ref[d, :, vs] = _dot(scores.astype(BF16), v_ref[:, vs])

    def step(i, carry):
        for d, (_, _, v_ref, _, _, _) in enumerate(dirs):
            r0 = pl.multiple_of((i if d == 0 else nchunk - 1 - i) * c, c)
            decay = jnp.exp(tot_ref[d, pl.ds(r0, 1), :])
            for h in range(nh):
                ks = slice(h * dk, (h + 1) * dk)
                vs = slice(h * dv, (h + 1) * dv)
                st = st_ref[d, h]
                oacc_ref[d, pl.ds(r0, c), vs] += _dot_nt(qd_ref[d, pl.ds(r0, c), ks], st.astype(BF16))
                vt = v_ref[pl.ds(r0, c), vs].astype(F32).T.astype(BF16)
                st_ref[d, h] = st * decay[:, ks] + _dot(vt, ke_ref[d, pl.ds(r0, c), ks])
        return carry

    lax.fori_loop(0, nchunk, step, 0, unroll=True)
    of_ref[...] = oacc_ref[0].astype(of_ref.dtype)
    ob_ref[...] = oacc_ref[1].astype(ob_ref.dtype)

    @pl.when(j == pl.num_programs(1) - 1)
    def _():
        sfin_ref[0] = st_ref[...]


def _gla(qk, v, gates, s0t, prev, *, row0, nb, t, tb, dk, dv):
    h = GLA_HEADS
    ntok = qk.shape[0]
    nblk = t // tb
    rb0 = row0 // tb
    kw, vw = h * dk, h * dv
    fwd = lambda b, j: rb0 + b * nblk + j
    bwd = lambda b, j: rb0 + b * nblk + nblk - 1 - j

    def views(rows, gate_col):
        return [pl.BlockSpec((tb, kw), lambda b, j: (rows(b, j), 0)),
                pl.BlockSpec((tb, kw), lambda b, j: (rows(b, j), 1)),
                pl.BlockSpec((tb, vw), lambda b, j: (rows(b, j), 0)),
                pl.BlockSpec((tb, kw), lambda b, j: (rows(b, j), gate_col))]

    extra = [] if prev is None else list(prev)
    state_spec = pl.BlockSpec((1, 2, h, dv, dk), lambda b, j: (b, 0, 0, 0, 0))
    return pl.pallas_call(
        functools.partial(_gla_kernel, nchunk=tb // GLA_CHUNK, dk=dk, dv=dv, nh=h),
        name="gla",
        input_output_aliases={9: 0, 10: 1} if extra else {},
        grid=(nb, nblk),
        in_specs=views(fwd, 0) + views(bwd, 1) + [state_spec] + [pl.BlockSpec(memory_space=pl.ANY)] * len(extra),
        out_specs=[
            pl.BlockSpec((tb, vw), lambda b, j: (fwd(b, j), 0)),
            pl.BlockSpec((tb, vw), lambda b, j: (bwd(b, j), 0)),
            state_spec,
        ],
        out_shape=[jax.ShapeDtypeStruct((ntok, vw), BF16),
                   jax.ShapeDtypeStruct((ntok, vw), BF16),
                   jax.ShapeDtypeStruct((nb, 2, h, dv, dk), F32)],
        scratch_shapes=[pltpu.VMEM((2, h, dv, dk), F32), pltpu.VMEM((2, tb, kw), BF16),
                        pltpu.VMEM((2, tb, kw), BF16), pltpu.VMEM((2, tb, kw), F32),
                        pltpu.VMEM((2, tb, vw), F32)],
        compiler_params=_cparams("parallel", "arbitrary"),
    )(qk, qk, v, gates, qk, qk, v, gates, s0t, *extra)


def _moe_kernel(idx_ref, gsm_ref, x_ref, mod_ref, gffn_ref, wg_ref, wu_ref, wd_ref, o_ref,
                xs_ref, gbuf_ref, ya_ref, yb_ref, *, cap, ne, nf):
    e = pl.program_id(1)
    f = pl.program_id(2)
    m = mod_ref[0]
    g2 = m[5:6]
    nsub = wg_ref.shape[1]
    rows_per = cap // (nf * nsub)
    e_next = jnp.minimum(e + 1, ne - 1)
    e_prev = jnp.maximum(e - 1, 0)

    def gather_row(en, r):
        gbuf_ref[pl.ds(r, 1), :] = x_ref[0, pl.ds(idx_ref[0, en, 0, r], 1), :]

    def scatter_row(ep, y_ref, r):
        i = idx_ref[0, ep, 0, r]
        o_ref[0, pl.ds(i, 1), :] += y_ref[pl.ds(r, 1), :] * gsm_ref[0, ep, 0, r] * g2

    @pl.when((e == 0) & (f == 0))
    def _():
        o_ref[...] = x_ref[...]
        yb_ref[...] = jnp.zeros(yb_ref.shape, F32)

        def body(r, carry):
            gather_row(0, r)
            return carry
        lax.fori_loop(0, cap, body, 0, unroll=8)

    @pl.when(f == 0)
    def _():
        xs_ref[...] = _rms_mod(gbuf_ref[...], gffn_ref[...], m[4:5], m[3:4]).astype(BF16)

    def expert_part(y_cur, y_prev):
        @pl.when(f == 0)
        def _():
            y_cur[...] = jnp.zeros(y_cur.shape, F32)

        def sub_chunk(k, carry):
            r0 = pl.multiple_of((f * nsub + k) * rows_per, rows_per)
            for r in range(rows_per):
                gather_row(e_next, r0 + r)
                scatter_row(e_prev, y_prev, r0 + r)
            xs = xs_ref[...]
            hid = _silu(_dot(xs, wg_ref[0, k])) * _dot(xs, wu_ref[0, k])
            y_cur[...] += _dot(hid.astype(BF16), wd_ref[0, k])
            return carry

        lax.fori_loop(0, nsub, sub_chunk, 0, unroll=True)

        @pl.when((e == ne - 1) & (f == nf - 1))
        def _():
            def body(r, carry):
                scatter_row(e, y_cur, r)
                return carry
            lax.fori_loop(0, cap, body, 0, unroll=8)

    @pl.when(e % 2 == 0)
    def _():
        expert_part(ya_ref, yb_ref)

    @pl.when(e % 2 == 1)
    def _():
        expert_part(yb_ref, ya_ref)


def _cast_kernel(w_ref, o_ref):
    o_ref[...] = w_ref[...].astype(BF16).reshape(o_ref.shape)


def _cast_cols_kernel(w_ref, o_ref):
    sub = o_ref.shape[-1]
    for k in range(o_ref.shape[2]):
        o_ref[0, 0, k] = w_ref[0, 0, :, k * sub:(k + 1) * sub].astype(BF16)


def _split_cast_cols(w, sub):
    nl, ne, d, ff = w.shape
    per = 1024 // sub
    return pl.pallas_call(
        _cast_cols_kernel,
        name="expert_weight_cols",
        grid=(nl, ne, ff // (per * sub)),
        in_specs=[pl.BlockSpec((1, 1, d, per * sub), lambda l, e, k: (l, e, 0, k))],
        out_specs=pl.BlockSpec((1, 1, per, d, sub), lambda l, e, k: (l, e, k, 0, 0)),
        out_shape=jax.ShapeDtypeStruct((nl, ne, ff // sub, d, sub), BF16),
        compiler_params=_cparams("parallel", "parallel", "parallel"),
    )(w)


def _split_cast_rows(w, sub):
    nl, ne, ff, d = w.shape
    rows = max(sub, 1024)
    return pl.pallas_call(
        _cast_kernel,
        name="expert_weight_rows",
        grid=(nl, ne, ff // rows),
        in_specs=[pl.BlockSpec((1, 1, rows, d), lambda l, e, k: (l, e, k, 0))],
        out_specs=pl.BlockSpec((1, 1, rows // sub, sub, d), lambda l, e, k: (l, e, k, 0, 0)),
        out_shape=jax.ShapeDtypeStruct((nl, ne, ff // sub, sub, d), BF16),
        compiler_params=_cparams("parallel", "parallel", "parallel"),
    )(w)


def _moe(x3, idx, gates, mod, g_ffn, w_gate, w_up, w_down, *, layer, nf):
    ns, ts, d = x3.shape
    _, ne, nsub_all, _, sub = w_gate.shape
    cap = idx.shape[2]
    nsub = nsub_all // nf
    return pl.pallas_call(
        functools.partial(_moe_kernel, cap=cap, ne=ne, nf=nf),
        name="moe",
        grid=(ns, ne, nf),
        in_specs=[
            pl.BlockSpec((1, ne, 1, cap), lambda s, e, f: (s, 0, 0, 0), memory_space=pltpu.SMEM),
            pl.BlockSpec((1, ne, 1, cap), lambda s, e, f: (s, 0, 0, 0), memory_space=pltpu.SMEM),
            pl.BlockSpec((1, ts, d), lambda s, e, f: (s, 0, 0), pipeline_mode=pl.Buffered(1)),
            pl.BlockSpec((1, 8, d), lambda s, e, f: (s, 0, 0)),
            pl.BlockSpec((1, d), lambda s, e, f: (0, 0)),
            pl.BlockSpec((None, 1, nsub, d, sub), lambda s, e, f: (layer, e, f, 0, 0)),
            pl.BlockSpec((None, 1, nsub, d, sub), lambda s, e, f: (layer, e, f, 0, 0)),
            pl.BlockSpec((None, 1, nsub, sub, d), lambda s, e, f: (layer, e, f, 0, 0)),
        ],
        out_specs=pl.BlockSpec((1, ts, d), lambda s, e, f: (s, 0, 0), pipeline_mode=pl.Buffered(1)),
        out_shape=jax.ShapeDtypeStruct((ns, ts, d), F32),
        scratch_shapes=[pltpu.VMEM((cap, d), BF16), pltpu.VMEM((cap, d), F32),
                        pltpu.VMEM((cap, d), F32), pltpu.VMEM((cap, d), F32)],
        compiler_params=_cparams("parallel", "arbitrary", "arbitrary"),
    )(idx.reshape(ns, ne, 1, cap), gates.reshape(ns, ne, 1, cap), x3, mod, g_ffn, w_gate, w_up, w_down)


def _route(aff, *, n_dec, ts, batch, seq, ne):
    cap_dec = EC_CAPACITY * ts // ne
    cap_ctx = EC_CAPACITY * seq // ne
    a = aff[:, :ne]
    a_dec = a[:n_dec * ts].reshape(n_dec, ts, ne).swapaxes(1, 2)
    g_dec, i_dec = lax.top_k(a_dec, cap_dec)
    a_ctx = a[n_dec * ts:].reshape(batch, seq, ne).swapaxes(1, 2)
    g_ctx, i_ctx = lax.top_k(a_ctx, cap_ctx)
    i_ctx = i_ctx + (jnp.arange(batch) * seq)[:, None, None]
    g_ctx = g_ctx.swapaxes(0, 1).reshape(1, ne, batch * cap_ctx)
    i_ctx = i_ctx.swapaxes(0, 1).reshape(1, ne, batch * cap_ctx)
    return (jnp.concatenate([i_dec, i_ctx], axis=0).astype(jnp.int32),
            jnp.concatenate([g_dec, g_ctx], axis=0))


def _final_kernel(x_ref, g_ref, o_ref):
    x = x_ref[...]
    ms = jnp.mean(x * x, axis=-1, keepdims=True)
    o_ref[...] = (x * lax.rsqrt(ms + RMS_EPS)) * g_ref[...]


def _final_norm(x, gain, *, row0, nrows, tm):
    d = x.shape[1]
    rb0 = row0 // tm
    return pl.pallas_call(
        _final_kernel,
        name="final_norm",
        grid=(nrows // tm,),
        in_specs=[pl.BlockSpec((tm, d), lambda i: (rb0 + i, 0)),
                  pl.BlockSpec((1, d), lambda i: (0, 0))],
        out_specs=pl.BlockSpec((tm, d), lambda i: (i, 0)),
        out_shape=jax.ShapeDtypeStruct((nrows, d), F32),
        compiler_params=_cparams("parallel"),
    )(x, gain)


def kernel(x_prompt, x_sample, c, c_ctx, cache_attn_k, cache_attn_v, state_s5, state_gla, norm_mix, norm_ffn, w_mod, b_mod, w_router, ffn_w_gate, ffn_w_up, ffn_w_down, ab_w_in, ab_w_out, attn_q_norm, attn_k_norm, s5_lambda_re, s5_lambda_im, s5_log_dt, s5_b_re, s5_b_im, s5_c_re, s5_c_im, s5_d, s5_w_glu, s5_b_glu, gla_w_in, gla_w_gate_up, gla_b_gate, gla_o_norm, gla_w_out, norm_out):
    batch, seq, d = x_prompt.shape
    n_dec, ts, _ = x_sample.shape
    assert batch * seq == ts, "context batch must flatten to one latent-length sequence"
    depth = w_mod.shape[0]
    assert depth == 2 and ab_w_in.shape[0] == 1 and gla_w_in.shape[0] == 1
    ns = n_dec + 1
    ntok = ns * ts
    n_dec_tok = n_dec * ts
    ne = w_router.shape[2]
    tm = min(TOKEN_TILE, ts)
    tps = ts // tm
    qw = ATTN_HEADS * HEAD_DIM
    kw = ATTN_KV_HEADS * HEAD_DIM
    s5w = d // 2
    s5g = s5w // S5_GROUP_CH
    s5n = s5_lambda_re.shape[-1]
    past = cache_attn_k.shape[2]

    x_dec = x_sample.reshape(n_dec_tok, d)
    x_ctx = x_prompt.reshape(batch * seq, d)

    rows = -(-ns // 8) * 8
    cond = jnp.concatenate([c, c_ctx[None, :], jnp.zeros((rows - ns, d), F32)], axis=0)
    mod = _modulation(cond, w_mod, b_mod)[:, :ns].reshape(depth, ns, 6, d)
    mod = jnp.pad(mod, ((0, 0), (0, 0), (0, 2), (0, 0)))

    wr_pad = jnp.pad(w_router, ((0, 0), (0, 0), (0, LANES - ne)))
    wg_bf = _split_cast_cols(ffn_w_gate, MOE_F_SUB)
    wu_bf = _split_cast_cols(ffn_w_up, MOE_F_SUB)
    wd_bf = _split_cast_rows(ffn_w_down, MOE_F_SUB)

    def moe_layer(xin, aff, layer):
        idx, gates = _route(aff, n_dec=n_dec, ts=ts, batch=batch, seq=seq, ne=ne)
        out = _moe(xin.reshape(ns, ts, d), idx, gates, mod[layer], norm_ffn[layer:layer + 1],
                   wg_bf, wu_bf, wd_bf, layer=layer, nf=MOE_F_PARTS)
        return out.reshape(ntok, d)

    proj, u = _inproj_ab(x_dec, x_ctx, mod[0], norm_mix[0:1], ab_w_in[0].astype(BF16),
                         n_qkv=qw + 2 * kw, tps=tps, tm=tm)
    cos_tab, sin_tab = _rope_tables(ts)
    q_bf, k_bf, k_nrm, v_bf = _qkprep(proj, cos_tab, sin_tab, attn_q_norm[0], attn_k_norm[0],
                                      ts=ts, tm=min(4 * tm, ts), n_rope_seq=n_dec)
    keys = jnp.concatenate([cache_attn_k[:, 0].reshape(n_dec, past, kw).astype(BF16),
                            k_bf[:n_dec_tok].reshape(n_dec, ts, kw)], axis=1)
    vals = jnp.concatenate([cache_attn_v[:, 0].reshape(n_dec, past, kw).astype(BF16),
                            v_bf[:n_dec_tok].reshape(n_dec, ts, kw)], axis=1)
    o_attn = _attention(q_bf, keys.transpose(0, 2, 1), vals, None, row0=0, t=ts, tq=min(ATTN_Q_TILE, ts))
    o_attn = _attention(q_bf, k_bf[n_dec_tok:].reshape(batch, seq, kw).transpose(0, 2, 1),
                        v_bf[n_dec_tok:].reshape(batch, seq, kw), o_attn,
                        row0=n_dec_tok, t=seq, tq=min(ATTN_Q_TILE, seq))

    mats = _s5_matrices(s5_lambda_re[0], s5_lambda_im[0], s5_log_dt[0], s5_b_re[0], s5_b_im[0],
                        s5_c_re[0], s5_c_im[0], s5_d[0])
    h0 = state_s5[:, 0].transpose(2, 1, 4, 0, 3).reshape(s5g, 4, n_dec, s5n)
    h0 = jnp.pad(h0, ((0, 0), (0, 0), (0, 0), (0, LANES - s5n)))
    seqs = ((n_dec, ts // S5_CHUNK, 0, True), (batch, seq // S5_CHUNK, n_dec_tok // S5_CHUNK, False))
    y, hfin = _s5(u, *mats, h0, p=S5_GROUP_CH, seqs=seqs)

    w_out = ab_w_out[0].astype(BF16)
    x, aff = _ab_post(y, o_attn, x_dec, x_ctx, mod[0], s5_w_glu[0].astype(BF16), s5_b_glu[0][None, :],
                      w_out[:qw], w_out[qw:], norm_ffn[0:1], wr_pad[0], tps=tps, tm=tm, ne=ne)
    x = moe_layer(x, aff, 0)

    nh = GLA_HEADS
    dk = d // 2 // nh
    dv = d // nh
    main_w = 2 * nh * dk + 2 * nh * dv
    w_gla = gla_w_in[0]
    wl = jnp.pad(w_gla[:, main_w:], ((0, 0), (0, LANES - 2 * GLA_RANK))).astype(BF16)
    wgu = jnp.zeros((LANES, 2 * nh * dk), F32)
    wgu = wgu.at[:GLA_RANK, :nh * dk].set(gla_w_gate_up[0, 0])
    wgu = wgu.at[GLA_RANK:2 * GLA_RANK, nh * dk:].set(gla_w_gate_up[0, 1])
    bgu = jnp.concatenate([gla_b_gate[0, 0], gla_b_gate[0, 1]])[None, :]
    qk1, v1, r1, gates1 = _inproj_gla(x, mod[1], norm_mix[1:2], w_gla[:, :main_w].astype(BF16), wl, wgu, bgu,
                                      widths=(2 * nh * dk, nh * dv, nh * dv), tps=tps, tm=tm)
    s0t = state_gla[:, 0].swapaxes(-1, -2)
    o_f, o_b, _ = _gla(qk1, v1, gates1, s0t, None, row0=0, nb=n_dec, t=ts, tb=min(GLA_BLOCK, ts), dk=dk, dv=dv)
    o_f, o_b, sfin = _gla(qk1, v1, gates1, jnp.zeros((batch, 2, nh, dv, dk), F32), (o_f, o_b),
                          row0=n_dec_tok, nb=batch, t=seq, tb=min(GLA_BLOCK, seq), dk=dk, dv=dv)
    x, aff = _gla_post(o_f, o_b, r1, x, mod[1], gla_o_norm[0][None, :], gla_w_out[0].astype(BF16),
                       norm_ffn[1:2], wr_pad[1], tps=tps, tm=tm, ne=ne)
    x = moe_layer(x, aff, 1)

    gain = norm_out[None, :]
    tn = min(4 * tm, ts)
    y_sample = _final_norm(x, gain, row0=0, nrows=n_dec_tok, tm=tn).reshape(n_dec, ts, d)
    y_prompt = _final_norm(x, gain, row0=n_dec_tok, nrows=batch * seq, tm=tn).reshape(batch, seq, d)
    new_k = k_nrm[n_dec_tok:].reshape(batch, 1, seq, ATTN_KV_HEADS, HEAD_DIM)
    new_v = proj[n_dec_tok:, qw + kw:qw + 2 * kw].reshape(batch, 1, seq, ATTN_KV_HEADS, HEAD_DIM)
    new_s5 = hfin[..., :s5n].reshape(s5g, 2, 2, batch, s5n).transpose(3, 1, 0, 4, 2)[:, None]
    new_gla = sfin.swapaxes(-1, -2)[:, None]
    return (y_prompt, y_sample, new_k, new_v, new_s5, new_gla)
```

```python
import functools
import math

import jax
import jax.numpy as jnp
from jax import lax
from jax.experimental import pallas as pl
from jax.experimental.pallas import tpu as pltpu

F32 = jnp.float32
BF16 = jnp.bfloat16

RMS_EPS = 1e-6
GRID_W = 64
ROPE_THETA = 10000.0
ATTN_HEADS = 8
ATTN_KV_HEADS = 2
HEAD_DIM = 64
S5_GROUP_CH = 16
S5_CHUNK = 16
GLA_HEADS = 4
GLA_RANK = 16
GLA_GATE_NORM = 16.0
GLA_CHUNK = 64
GLA_BLOCK = 256
EC_CAPACITY = 2
MOE_F_SUB = 512
MOE_F_PARTS = 2
LANES = 128
TOKEN_TILE = 256
ATTN_Q_TILE = 256
VMEM_LIMIT = 56 * 1024 * 1024


def _cparams(*sem):
    return pltpu.CompilerParams(dimension_semantics=sem, vmem_limit_bytes=VMEM_LIMIT)


def _dot(a, b):
    return jnp.dot(a, b, preferred_element_type=F32)


def _dot_nt(a, b):
    return lax.dot_general(a, b, (((1,), (1,)), ((), ())), preferred_element_type=F32)


def _split(a):
    hi = a.astype(BF16)
    lo = (a - hi.astype(F32)).astype(BF16)
    return hi, lo


def _dot3(a, b):
    ah, al = _split(a)
    bh, bl = _split(b)
    return _dot(ah, bh) + _dot(al, bh) + _dot(ah, bl)


def _dot2(a, b_exact):
    ah, al = _split(a)
    return _dot(ah, b_exact) + _dot(al, b_exact)


def _sigmoid(x):
    return 1.0 / (1.0 + jnp.exp(-x))


def _silu(x):
    return x * _sigmoid(x)


def _gelu_tanh(x):
    c = math.sqrt(2.0 / math.pi)
    return 0.5 * x * (1.0 + jnp.tanh(c * (x + 0.044715 * (x * x * x))))


def _log_sigmoid(z):
    return jnp.minimum(z, 0.0) - jnp.log1p(jnp.exp(-jnp.abs(z)))


def _rms_mod(x, gain, scale, shift):
    ms = jnp.mean(x * x, axis=-1, keepdims=True)
    return (x * lax.rsqrt(ms + RMS_EPS)) * gain * (1.0 + scale) + shift


def _router_probs(xn, mod, gain, wr, ne):
    h = _rms_mod(xn, gain, mod[4:5], mod[3:4])
    logits = _dot3(h, wr)
    lane = lax.broadcasted_iota(jnp.int32, logits.shape, 1)
    logits = jnp.where(lane < ne, logits, -jnp.inf)
    m = jnp.max(logits, axis=-1, keepdims=True)
    p = jnp.exp(logits - m)
    return p / jnp.sum(p, axis=-1, keepdims=True)


def _mod_kernel(c_ref, w_ref, b_ref, o_ref):
    c = c_ref[...]
    o_ref[0] = _dot3(_silu(c), w_ref[0]) + b_ref[0]


def _modulation(cond, w_mod, b_mod):
    depth, d, n = w_mod.shape
    r = cond.shape[0]
    tn = n // 4
    return pl.pallas_call(
        _mod_kernel,
        name="modulation",
        grid=(depth, n // tn),
        in_specs=[
            pl.BlockSpec((r, d), lambda l, j: (0, 0)),
            pl.BlockSpec((1, d, tn), lambda l, j: (l, 0, j)),
            pl.BlockSpec((1, 1, tn), lambda l, j: (l, 0, j)),
        ],
        out_specs=pl.BlockSpec((1, r, tn), lambda l, j: (l, 0, j)),
        out_shape=jax.ShapeDtypeStruct((depth, r, n), F32),
        compiler_params=_cparams("parallel", "parallel"),
    )(cond, w_mod, b_mod.reshape(depth, 1, n))


def _two_group_specs(tm, d, nd_tiles):
    return [pl.BlockSpec((tm, d), lambda i: (jnp.minimum(i, nd_tiles - 1), 0)),
            pl.BlockSpec((tm, d), lambda i: (jnp.maximum(i - nd_tiles, 0), 0))]


def _two_group_tile(xd_ref, xc_ref, nd_tiles):
    return jnp.where(pl.program_id(0) < nd_tiles, xd_ref[...], xc_ref[...])


def _to_step_major(tok_ref, out_ref):
    nq, steps, nrow, _ = out_ref.shape
    for q in range(nq):
        for i in range(steps):
            out_ref[q, i] = tok_ref[q, pl.ds(i, nrow, stride=steps), :].astype(out_ref.dtype)


def _from_step_major(in_ref, tok_ref):
    nq, steps, nrow, _ = in_ref.shape
    for q in range(nq):
        for i in range(steps):
            tok_ref[q, pl.ds(i, nrow, stride=steps), :] = in_ref[q, i].astype(F32)


def _qk_norm_rope(x, gain, ones, cos, sin):
    width = x.shape[1]
    ssq = _dot2(x * x, ones)
    xn = x * lax.rsqrt(ssq * (1.0 / HEAD_DIM) + RMS_EPS) * gain
    lane = lax.broadcasted_iota(jnp.int32, x.shape, 1)
    first = (lane & 16) == 0
    partner = jnp.where(first, pltpu.roll(xn, width - 16, 1), pltpu.roll(xn, 16, 1))
    rep = width // LANES
    cosw = jnp.concatenate([cos] * rep, axis=1) if rep > 1 else cos
    sinw = jnp.concatenate([sin] * rep, axis=1) if rep > 1 else sin
    return xn, xn * cosw + partner * sinw


def _inproj_ab_kernel(xd_ref, xc_ref, mod_ref, g_ref, w_ref, cos_ref, sin_ref, qg_ref, kg_ref, oq_ref, ok_ref,
                      qo_ref, ko_ref, kn_ref, vo_ref, vf_ref, u_ref, scr_ref, *, nd_tiles):
    m = mod_ref[0]
    x = _two_group_tile(xd_ref, xc_ref, nd_tiles)
    h = _rms_mod(x, g_ref[...], m[1:2], m[0:1]).astype(BF16)
    qw, kw = qo_ref.shape[1], ko_ref.shape[1]
    nq = qw + 2 * kw
    qkv = _dot(h, w_ref[:, :nq])
    cos = cos_ref[0]
    sin = sin_ref[0]
    _, qr = _qk_norm_rope(qkv[:, :qw], qg_ref[...], oq_ref[...], cos, sin)
    qo_ref[...] = (qr * (HEAD_DIM ** -0.5)).astype(BF16)
    kn, kr = _qk_norm_rope(qkv[:, qw:qw + kw], kg_ref[...], ok_ref[...], cos, sin)
    kn_ref[...] = kn
    ko_ref[...] = kr.astype(BF16)
    v = qkv[:, qw + kw:]
    vf_ref[...] = v
    vo_ref[...] = v.astype(BF16)
    u = _dot(h, w_ref[:, nq:])
    for q in range(scr_ref.shape[0]):
        scr_ref[q] = u[:, q * LANES:(q + 1) * LANES]
    _to_step_major(scr_ref, u_ref)


def _inproj_ab(x_dec, x_ctx, mod, gain, w, cos_tab, sin_tab, q_gain, k_gain, *, ts, n_rope_seq, tm):
    d = x_dec.shape[1]
    nd_tiles = x_dec.shape[0] // tm
    ntok = x_dec.shape[0] + x_ctx.shape[0]
    n = w.shape[1]
    qw = ATTN_HEADS * HEAD_DIM
    kw = ATTN_KV_HEADS * HEAD_DIM
    uw = n - qw - 2 * kw
    tps = ts // tm

    def head_ones(width):
        a = jnp.arange(width) // HEAD_DIM
        return (a[:, None] == a[None, :]).astype(BF16)

    def tab_map(i):
        return (jnp.minimum(i // tps, n_rope_seq) // n_rope_seq, i % tps, 0)

    const = lambda shape: pl.BlockSpec(shape, lambda i: (0,) * len(shape))
    row = lambda width: pl.BlockSpec((tm, width), lambda i: (i, 0))
    return pl.pallas_call(
        functools.partial(_inproj_ab_kernel, nd_tiles=nd_tiles),
        name="inproj_ab",
        grid=(ntok // tm,),
        in_specs=_two_group_specs(tm, d, nd_tiles) + [
            pl.BlockSpec((1, 8, d), lambda i: (i // tps, 0, 0)),
            const((1, d)), const((d, n)),
            pl.BlockSpec((1, tm, LANES), tab_map),
            pl.BlockSpec((1, tm, LANES), tab_map),
            const((1, qw)), const((1, kw)), const((qw, qw)), const((kw, kw)),
        ],
        out_specs=[row(qw), row(kw), row(kw), row(kw), row(kw),
                   pl.BlockSpec((uw // LANES, S5_CHUNK, tm // S5_CHUNK, LANES), lambda i: (0, 0, i, 0))],
        out_shape=[jax.ShapeDtypeStruct((ntok, qw), BF16),
                   jax.ShapeDtypeStruct((ntok, kw), BF16),
                   jax.ShapeDtypeStruct((ntok, kw), F32),
                   jax.ShapeDtypeStruct((ntok, kw), BF16),
                   jax.ShapeDtypeStruct((ntok, kw), F32),
                   jax.ShapeDtypeStruct((uw // LANES, S5_CHUNK, ntok // S5_CHUNK, LANES), BF16)],
        scratch_shapes=[pltpu.VMEM((uw // LANES, tm, LANES), F32)],
        compiler_params=_cparams("parallel"),
    )(x_dec, x_ctx, mod, gain, w, cos_tab, sin_tab,
      jnp.tile(q_gain, ATTN_HEADS).reshape(1, qw), jnp.tile(k_gain, ATTN_KV_HEADS).reshape(1, kw),
      head_ones(qw), head_ones(kw))


def _inproj_gla_kernel(x_ref, mod_ref, g_ref, w_ref, wl_ref, wg_ref, bg_ref, qk_ref, v_ref, r_ref, gate_ref):
    m = mod_ref[0]
    h = _rms_mod(x_ref[...], g_ref[...], m[1:2], m[0:1]).astype(BF16)
    c0 = 0
    for ref in (qk_ref, v_ref, r_ref):
        w = ref.shape[1]
        for s in range(0, w, 512):
            ref[:, s:s + 512] = _dot(h, w_ref[:, c0 + s:c0 + s + 512]).astype(ref.dtype)
        c0 += w
    low = _dot(h, wl_ref[...])
    z = _dot3(low, wg_ref[...]) + bg_ref[...]
    gate_ref[...] = (_log_sigmoid(z) / GLA_GATE_NORM).astype(gate_ref.dtype)


def _inproj_gla(x, mod, gain, w, wl, wg, bg, *, widths, tps, tm):
    ntok, d = x.shape
    n = w.shape[1]
    ng = wg.shape[1]
    wqk, wv, wr = widths
    row = lambda width: pl.BlockSpec((tm, width), lambda i: (i, 0))
    return pl.pallas_call(
        _inproj_gla_kernel,
        name="inproj_gla",
        grid=(ntok // tm,),
        in_specs=[
            pl.BlockSpec((tm, d), lambda i: (i, 0)),
            pl.BlockSpec((1, 8, d), lambda i: (i // tps, 0, 0)),
            pl.BlockSpec((1, d), lambda i: (0, 0)),
            pl.BlockSpec((d, n), lambda i: (0, 0)),
            pl.BlockSpec((d, LANES), lambda i: (0, 0)),
            pl.BlockSpec((LANES, ng), lambda i: (0, 0)),
            pl.BlockSpec((1, ng), lambda i: (0, 0)),
        ],
        out_specs=[row(wqk), row(wv), row(wr), row(ng)],
        out_shape=[jax.ShapeDtypeStruct((ntok, wqk), BF16),
                   jax.ShapeDtypeStruct((ntok, wv), BF16),
                   jax.ShapeDtypeStruct((ntok, wr), BF16),
                   jax.ShapeDtypeStruct((ntok, ng), BF16)],
        compiler_params=_cparams("parallel"),
    )(x, mod, gain, w, wl, wg, bg)


def _rope_tables(t):
    nfreq = HEAD_DIM // 4
    inv = ROPE_THETA ** (-jnp.arange(nfreq, dtype=F32) / nfreq)
    pos = jnp.arange(t)
    row = (pos // GRID_W).astype(F32)[:, None] * inv[None, :]
    col = (pos % GRID_W).astype(F32)[:, None] * inv[None, :]
    cos = jnp.concatenate([jnp.cos(row), jnp.cos(row), jnp.cos(col), jnp.cos(col)], axis=1)
    sin = jnp.concatenate([-jnp.sin(row), jnp.sin(row), -jnp.sin(col), jnp.sin(col)], axis=1)
    rep = LANES // HEAD_DIM
    cos = jnp.tile(cos, (1, rep))
    sin = jnp.tile(sin, (1, rep))
    return (jnp.stack([cos, jnp.ones_like(cos)]), jnp.stack([sin, jnp.zeros_like(sin)]))


def _attn_kernel(q_ref, kt_ref, v_ref, *rest):
    o_ref = rest[-1]
    q = q_ref[...]
    v = v_ref[0]
    group = ATTN_HEADS // ATTN_KV_HEADS
    outs = []
    for h in range(ATTN_HEADS):
        kv = h // group
        qh = q[:, h * HEAD_DIM:(h + 1) * HEAD_DIM]
        kt = kt_ref[0, kv * HEAD_DIM:(kv + 1) * HEAD_DIM, :]
        s = _dot(qh, kt)
        p = jnp.exp(s - jnp.max(s, axis=-1, keepdims=True))
        l = jnp.sum(p, axis=-1, keepdims=True)
        o = _dot(p.astype(BF16), v)
        outs.append(o[:, kv * HEAD_DIM:(kv + 1) * HEAD_DIM] / l)
    o_ref[...] = jnp.concatenate(outs, axis=1).astype(BF16)


def _attention(q, keys_t, vals, prev, *, row0, t, tq):
    b, kw, s = keys_t.shape
    ntok, qw = q.shape
    nq = t // tq
    rb0 = row0 // tq
    extra = [] if prev is None else [prev]
    return pl.pallas_call(
        _attn_kernel,
        name="attention",
        grid=(b, nq),
        in_specs=[
            pl.BlockSpec((tq, qw), lambda bi, i: (rb0 + bi * nq + i, 0)),
            pl.BlockSpec((1, kw, s), lambda bi, i: (bi, 0, 0)),
            pl.BlockSpec((1, s, kw), lambda bi, i: (bi, 0, 0)),
        ] + [pl.BlockSpec(memory_space=pl.ANY)] * len(extra),
        out_specs=pl.BlockSpec((tq, qw), lambda bi, i: (rb0 + bi * nq + i, 0)),
        out_shape=jax.ShapeDtypeStruct((ntok, qw), BF16),
        input_output_aliases={3: 0} if extra else {},
        compiler_params=_cparams("parallel", "parallel"),
    )(q, keys_t, vals, *extra)


def _s5_kernel(u_ref, e_ref, m_ref, f_ref, lam_ref, h0_ref, y_ref, hfin_ref, xg_ref, s_ref, hin_ref, *, p, seqs):
    steps, nr = u_ref.shape[1], u_ref.shape[2]
    gpb = LANES // p
    lane_blk = lax.broadcasted_iota(jnp.int32, (1, LANES), 1) // p
    y_ref[...] = jnp.zeros(y_ref.shape, y_ref.dtype)

    def group(g8, carry):
        halves = []
        for half in range(steps // gpb):
            acc = jnp.zeros((nr, LANES), F32)
            for j in range(gpb):
                slab = u_ref[0, half * gpb + j].astype(F32)
                rolled = pltpu.roll(slab, (p * j + LANES - p * g8) & (LANES - 1), 1)
                acc = jnp.where(lane_blk == j, rolled, acc)
            halves.append(acc)
        xg_ref[...] = jnp.concatenate(halves, axis=1).astype(BF16)
        xg = xg_ref[...]
        eh, el = _split(e_ref[g8])
        ends = _dot(xg, eh) + _dot(xg, el)
        for k in range(4):
            s_ref[k] = ends[:, k * LANES:(k + 1) * LANES]
        lam = lam_ref[g8]

        for nb, nc, row0, carries in seqs:
            lfr = jnp.broadcast_to(lam[0:1], (nb, LANES))
            lfi = jnp.broadcast_to(lam[1:2], (nb, LANES))
            lbr = jnp.broadcast_to(lam[2:3], (nb, LANES))
            lbi = jnp.broadcast_to(lam[3:4], (nb, LANES))

            def step(c, carry, nb=nb, nc=nc, row0=row0, lfr=lfr, lfi=lfi, lbr=lbr, lbi=lbi):
                hfr, hfi, hbr, hbi = carry
                rf = pl.ds(row0 + c, nb, stride=nc)
                rb = pl.ds(row0 + nc - 1 - c, nb, stride=nc)
                hin_ref[0, rf, :] = hfr
                hin_ref[1, rf, :] = hfi
                hin_ref[2, rb, :] = hbr
                hin_ref[3, rb, :] = hbi
                sfr = s_ref[0, rf, :]
                sfi = s_ref[1, rf, :]
                sbr = s_ref[2, rb, :]
                sbi = s_ref[3, rb, :]
                return (lfr * hfr - lfi * hfi + sfr, lfr * hfi + lfi * hfr + sfi,
                        lbr * hbr - lbi * hbi + sbr, lbr * hbi + lbi * hbr + sbi)

            if carries:
                h0 = h0_ref[g8]
                init = (h0[0], h0[1], h0[2], h0[3])
            else:
                init = (jnp.zeros((nb, LANES), F32),) * 4
            fin = lax.fori_loop(0, nc, step, init, unroll=4)
            if not carries:
                for k in range(4):
                    hfin_ref[g8, k] = fin[k]

        mh, ml = _split(m_ref[g8])
        y = _dot(xg, mh) + _dot(xg, ml)
        for k in range(0, 4, 2):
            hin = jnp.concatenate([hin_ref[k], hin_ref[k + 1]], axis=1)
            y = y + _dot3(hin, f_ref[g8, k * LANES:(k + 2) * LANES, :])
        for half in range(steps // gpb):
            piece = y[:, half * LANES:(half + 1) * LANES]
            for j in range(gpb):
                rolled = pltpu.roll(piece, (p * g8 + LANES - p * j) & (LANES - 1), 1)
                i = half * gpb + j
                y_ref[0, i] = jnp.where(lane_blk == g8, rolled.astype(y_ref.dtype), y_ref[0, i])
        return carry

    lax.fori_loop(0, gpb, group, 0)


def _s5(u5, e_mat, m_mat, f_mat, lam_l, h0, *, p, seqs):
    nq, steps, nr, _ = u5.shape
    gpb = LANES // p
    w = steps * p
    sw = 4 * LANES
    nb_carry = h0.shape[2]
    nb_free = [s[0] for s in seqs if not s[3]][0]
    once = pl.Buffered(1)
    return pl.pallas_call(
        functools.partial(_s5_kernel, p=p, seqs=seqs),
        name="s5",
        grid=(nq,),
        in_specs=[
            pl.BlockSpec((1, steps, nr, LANES), lambda q: (q, 0, 0, 0), pipeline_mode=once),
            pl.BlockSpec((gpb, w, sw), lambda q: (q, 0, 0), pipeline_mode=once),
            pl.BlockSpec((gpb, w, w), lambda q: (q, 0, 0), pipeline_mode=once),
            pl.BlockSpec((gpb, sw, w), lambda q: (q, 0, 0), pipeline_mode=once),
            pl.BlockSpec((gpb, 8, LANES), lambda q: (q, 0, 0)),
            pl.BlockSpec((gpb, 4, nb_carry, LANES), lambda q: (q, 0, 0, 0)),
        ],
        out_specs=[pl.BlockSpec((1, steps, nr, LANES), lambda q: (q, 0, 0, 0), pipeline_mode=once),
                   pl.BlockSpec((gpb, 4, nb_free, LANES), lambda q: (q, 0, 0, 0))],
        out_shape=[jax.ShapeDtypeStruct(u5.shape, BF16),
                   jax.ShapeDtypeStruct((nq * gpb, 4, nb_free, LANES), F32)],
        scratch_shapes=[pltpu.VMEM((nr, w), BF16), pltpu.VMEM((4, nr, LANES), F32), pltpu.VMEM((4, nr, LANES), F32)],
        compiler_params=_cparams("parallel"),
    )(u5, e_mat, m_mat, f_mat, lam_l, h0)


def _s5_matrices(lam_re, lam_im, log_dt, b_re, b_im, c_re, c_im, d_skip):
    hi = lax.Precision.HIGHEST
    L = S5_CHUNK
    _, g, n = lam_re.shape
    p = b_re.shape[-1]
    dt = jnp.exp(log_dt)[..., None]
    mag = jnp.exp(lam_re * dt)
    lbr, lbi = mag * jnp.cos(lam_im * dt), mag * jnp.sin(lam_im * dt)
    den = lam_re * lam_re + lam_im * lam_im
    cr = ((lbr - 1.0) * lam_re + lbi * lam_im) / den
    ci = (lbi * lam_re - (lbr - 1.0) * lam_im) / den
    bbr = cr[..., None] * b_re - ci[..., None] * b_im
    bbi = cr[..., None] * b_im + ci[..., None] * b_re
    pr, pi = [jnp.ones_like(lbr)], [jnp.zeros_like(lbr)]
    for _ in range(L):
        pr, pi = pr + [pr[-1] * lbr - pi[-1] * lbi], pi + [pr[-1] * lbi + pi[-1] * lbr]
    pwr, pwi = jnp.stack(pr, axis=2), jnp.stack(pi, axis=2)
    cpr = c_re[:, :, None] * pwr[:, :, :, None, :] - c_im[:, :, None] * pwi[:, :, :, None, :]
    cpi = c_re[:, :, None] * pwi[:, :, :, None, :] + c_im[:, :, None] * pwr[:, :, :, None, :]
    kern = (jnp.einsum('dgkpn,dgnq->dgkpq', cpr[:, :, :L], bbr, precision=hi)
            - jnp.einsum('dgkpn,dgnq->dgkpq', cpi[:, :, :L], bbi, precision=hi))
    step = jnp.arange(L)
    lag_f = step[None, :] - step[:, None]
    lag_b = -lag_f

    def toeplitz(kd, lag):
        t = jnp.take(kd, jnp.clip(lag, 0, L - 1), axis=1)
        t = jnp.where((lag >= 0)[None, :, :, None, None], t, 0.0)
        return t.transpose(0, 1, 4, 2, 3).reshape(g, L * p, L * p)

    eye_l = jnp.eye(L, dtype=F32)
    dmat = jnp.einsum('ji,gq,qp->gjqip', eye_l, d_skip, jnp.eye(p, dtype=F32)).reshape(g, L * p, L * p)
    m_mat = toeplitz(kern[0], lag_f) + toeplitz(kern[1], lag_b) + dmat

    def e_block(pw_r, pw_i, b_r, b_i):
        er = pw_r[:, :, None, :] * b_r.transpose(0, 2, 1)[:, None] - pw_i[:, :, None, :] * b_i.transpose(0, 2, 1)[:, None]
        ei = pw_r[:, :, None, :] * b_i.transpose(0, 2, 1)[:, None] + pw_i[:, :, None, :] * b_r.transpose(0, 2, 1)[:, None]
        return er.reshape(g, L * p, n), ei.reshape(g, L * p, n)

    def pad_cols(a):
        return jnp.pad(a, ((0, 0), (0, 0), (0, LANES - n)))

    efr, efi = e_block(pwr[0][:, L - 1 - step], pwi[0][:, L - 1 - step], bbr[0], bbi[0])
    ebr, ebi = e_block(pwr[1][:, step], pwi[1][:, step], bbr[1], bbi[1])
    e_mat = jnp.concatenate([pad_cols(efr), pad_cols(efi), pad_cols(ebr), pad_cols(ebi)], axis=2)

    def f_block(c_r, c_i):
        fr = c_r.transpose(0, 3, 1, 2).reshape(g, n, L * p)
        fi = -c_i.transpose(0, 3, 1, 2).reshape(g, n, L * p)
        return fr, fi

    def pad_rows(a):
        return jnp.pad(a, ((0, 0), (0, LANES - n), (0, 0)))

    ffr, ffi = f_block(cpr[0][:, step + 1], cpi[0][:, step + 1])
    fbr, fbi = f_block(cpr[1][:, L - step], cpi[1][:, L - step])
    f_mat = jnp.concatenate([pad_rows(ffr), pad_rows(ffi), pad_rows(fbr), pad_rows(fbi)], axis=1)

    lam_l = jnp.stack([pwr[0, :, L], pwi[0, :, L], pwr[1, :, L], pwi[1, :, L]], axis=1)
    lam_l = jnp.pad(lam_l, ((0, 0), (0, 4), (0, LANES - n)))
    return e_mat, m_mat, f_mat, lam_l


def _ab_post_kernel(y_ref, oa_ref, xd_ref, xc_ref, mod_ref, wglu_ref, bglu_ref, woa_ref, wos_ref, gffn_ref,
                    wr_ref, xo_ref, aff_ref, ytok_ref, *, ne, nd_tiles):
    m = mod_ref[0]
    half = wos_ref.shape[0]
    _from_step_major(y_ref, ytok_ref)
    y = jnp.concatenate([ytok_ref[q] for q in range(ytok_ref.shape[0])], axis=1)
    glu = _dot(_gelu_tanh(y).astype(BF16), wglu_ref[...]) + bglu_ref[...]
    o_s5 = glu[:, :half] * _sigmoid(glu[:, half:])
    out = _dot(oa_ref[...], woa_ref[...]) + _dot(o_s5.astype(BF16), wos_ref[...])
    xn = _two_group_tile(xd_ref, xc_ref, nd_tiles) + m[2:3] * out
    xo_ref[...] = xn
    aff_ref[...] = _router_probs(xn, m, gffn_ref[...], wr_ref[...], ne)


def _ab_post(y, o_attn, x_dec, x_ctx, mod, w_glu, b_glu, w_out_a, w_out_s, g_ffn, w_router, *, tps, tm, ne):
    d = x_dec.shape[1]
    nd_tiles = x_dec.shape[0] // tm
    ntok = x_dec.shape[0] + x_ctx.shape[0]
    nq, steps, _, _ = y.shape
    full = lambda shape: pl.BlockSpec(shape, lambda i: (0,) * len(shape))
    return pl.pallas_call(
        functools.partial(_ab_post_kernel, ne=ne, nd_tiles=nd_tiles),
        name="ab_post",
        grid=(ntok // tm,),
        scratch_shapes=[pltpu.VMEM((nq, tm, LANES), F32)],
        in_specs=[
            pl.BlockSpec((nq, steps, tm // steps, LANES), lambda i: (0, 0, i, 0)),
            pl.BlockSpec((tm, o_attn.shape[1]), lambda i: (i, 0)),
        ] + _two_group_specs(tm, d, nd_tiles) + [
            pl.BlockSpec((1, 8, d), lambda i: (i // tps, 0, 0)),
            full(w_glu.shape), full(b_glu.shape), full(w_out_a.shape), full(w_out_s.shape),
            full(g_ffn.shape), full(w_router.shape),
        ],
        out_specs=[pl.BlockSpec((tm, d), lambda i: (i, 0)),
                   pl.BlockSpec((tm, LANES), lambda i: (i, 0))],
        out_shape=[jax.ShapeDtypeStruct((ntok, d), F32),
                   jax.ShapeDtypeStruct((ntok, LANES), F32)],
        compiler_params=_cparams("parallel"),
    )(y, o_attn, x_dec, x_ctx, mod, w_glu, b_glu, w_out_a, w_out_s, g_ffn, w_router)


def _gla_post_kernel(of_ref, ob_ref, r_ref, x_ref, mod_ref, gn_ref, wo_ref, gffn_ref, wr_ref, xo_ref, aff_ref,
                     *, ne):
    m = mod_ref[0]
    o = of_ref[...].astype(F32) + ob_ref[...].astype(F32)
    dv = gn_ref.shape[1]
    parts = []
    for h in range(GLA_HEADS):
        oh = o[:, h * dv:(h + 1) * dv]
        ms = jnp.mean(oh * oh, axis=-1, keepdims=True)
        parts.append(oh * lax.rsqrt(ms + RMS_EPS) * gn_ref[...])
    on = jnp.concatenate(parts, axis=1) * _silu(r_ref[...].astype(F32))
    out = _dot(on.astype(BF16), wo_ref[...])
    xn = x_ref[...] + m[2:3] * out
    xo_ref[...] = xn
    aff_ref[...] = _router_probs(xn, m, gffn_ref[...], wr_ref[...], ne)


def _gla_post(o_fwd, o_bwd, r, x, mod, g_norm, w_out, g_ffn, w_router, *, tps, tm, ne):
    ntok, d = x.shape
    vw = o_fwd.shape[1]
    full = lambda shape: pl.BlockSpec(shape, lambda i: (0,) * len(shape))
    return pl.pallas_call(
        functools.partial(_gla_post_kernel, ne=ne),
        name="gla_post",
        grid=(ntok // tm,),
        in_specs=[
            pl.BlockSpec((tm, vw), lambda i: (i, 0)),
            pl.BlockSpec((tm, vw), lambda i: (i, 0)),
            pl.BlockSpec((tm, vw), lambda i: (i, 0)),
            pl.BlockSpec((tm, d), lambda i: (i, 0)),
            pl.BlockSpec((1, 8, d), lambda i: (i // tps, 0, 0)),
            full(g_norm.shape), full(w_out.shape), full(g_ffn.shape), full(w_router.shape),
        ],
        out_specs=[pl.BlockSpec((tm, d), lambda i: (i, 0)),
                   pl.BlockSpec((tm, LANES), lambda i: (i, 0))],
        out_shape=[jax.ShapeDtypeStruct((ntok, d), F32),
                   jax.ShapeDtypeStruct((ntok, LANES), F32)],
        compiler_params=_cparams("parallel"),
    )(o_fwd, o_bwd, r, x, mod, g_norm, w_out, g_ffn, w_router)


def _gla_kernel(qf_ref, kf_ref, vf_ref, gf_ref, qb_ref, kb_ref, vb_ref, gb_ref, s0_ref, *rest,
                nchunk, dk, dv, nh):
    of_ref, ob_ref, sfin_ref, st_ref, qd_ref, ke_ref, tot_ref, oacc_ref = rest[-8:]
    j = pl.program_id(1)
    c = GLA_CHUNK
    tb = nchunk * c

    @pl.when(j == 0)
    def _():
        st_ref[...] = s0_ref[0]

    row = lax.broadcasted_iota(jnp.int32, (tb, tb), 0)
    col = lax.broadcasted_iota(jnp.int32, (tb, tb), 1)
    same = (row // c) == (col // c)
    same_b = same.astype(BF16)
    dirs = ((qf_ref, kf_ref, vf_ref, gf_ref, of_ref, same & (row >= col)),
            (qb_ref, kb_ref, vb_ref, gb_ref, ob_ref, same & (row <= col)))
    for d, (q_ref, k_ref, v_ref, g_ref, o_ref, tri) in enumerate(dirs):
        g = g_ref[...]
        cum = _dot(tri.astype(BF16), g)
        tot = _dot(same_b, g)
        k = k_ref[...].astype(F32)
        q_dec = (q_ref[...].astype(F32) * (dk ** -0.5) * jnp.exp(cum)).astype(BF16)
        k_inv = (k * jnp.exp(-cum)).astype(BF16)
        qd_ref[d] = q_dec
        ke_ref[d] = (k * jnp.exp(tot - cum)).astype(BF16)
        tot_ref[d] = tot
        for h in range(nh):
            ks = slice(h * dk, (h + 1) * dk)
            vs = slice(h * dv, (h + 1) * dv)
            scores = jnp.where(tri, _dot_nt(q_dec[:, ks], k_inv[:, ks]), 0.0)
            oacc_ref[d, :, vs] = _dot(scores.astype(BF16), v_ref[:, vs])

    def step(i, carry):
        for d, (_, _, v_ref, _, _, _) in enumerate(dirs):
            r0 = pl.multiple_of((i if d == 0 else nchunk - 1 - i) * c, c)
            decay = jnp.exp(tot_ref[d, pl.ds(r0, 1), :])
            for h in range(nh):
                ks = slice(h * dk, (h + 1) * dk)
                vs = slice(h * dv, (h + 1) * dv)
                st = st_ref[d, h]
                oacc_ref[d, pl.ds(r0, c), vs] += _dot_nt(qd_ref[d, pl.ds(r0, c), ks], st.astype(BF16))
                vt = v_ref[pl.ds(r0, c), vs].astype(F32).T.astype(BF16)
                st_ref[d, h] = st * decay[:, ks] + _dot(vt, ke_ref[d, pl.ds(r0, c), ks])
        return carry

    lax.fori_loop(0, nchunk, step, 0, unroll=True)
    of_ref[...] = oacc_ref[0].astype(of_ref.dtype)
    ob_ref[...] = oacc_ref[1].astype(ob_ref.dtype)

    @pl.when(j == pl.num_programs(1) - 1)
    def _():
        sfin_ref[0] = st_ref[...]


def _gla(qk, v, gates, s0t, prev, *, row0, nb, t, tb, dk, dv):
    h = GLA_HEADS
    ntok = qk.shape[0]
    nblk = t // tb
    rb0 = row0 // tb
    kw, vw = h * dk, h * dv
    fwd = lambda b, j: rb0 + b * nblk + j
    bwd = lambda b, j: rb0 + b * nblk + nblk - 1 - j

    def views(rows, gate_col):
        return [pl.BlockSpec((tb, kw), lambda b, j: (rows(b, j), 0)),
                pl.BlockSpec((tb, kw), lambda b, j: (rows(b, j), 1)),
                pl.BlockSpec((tb, vw), lambda b, j: (rows(b, j), 0)),
                pl.BlockSpec((tb, kw), lambda b, j: (rows(b, j), gate_col))]

    extra = [] if prev is None else list(prev)
    state_spec = pl.BlockSpec((1, 2, h, dv, dk), lambda b, j: (b, 0, 0, 0, 0))
    return pl.pallas_call(
        functools.partial(_gla_kernel, nchunk=tb // GLA_CHUNK, dk=dk, dv=dv, nh=h),
        name="gla",
        input_output_aliases={9: 0, 10: 1} if extra else {},
        grid=(nb, nblk),
        in_specs=views(fwd, 0) + views(bwd, 1) + [state_spec] + [pl.BlockSpec(memory_space=pl.ANY)] * len(extra),
        out_specs=[
            pl.BlockSpec((tb, vw), lambda b, j: (fwd(b, j), 0)),
            pl.BlockSpec((tb, vw), lambda b, j: (bwd(b, j), 0)),
            state_spec,
        ],
        out_shape=[jax.ShapeDtypeStruct((ntok, vw), BF16),
                   jax.ShapeDtypeStruct((ntok, vw), BF16),
                   jax.ShapeDtypeStruct((nb, 2, h, dv, dk), F32)],
        scratch_shapes=[pltpu.VMEM((2, h, dv, dk), F32), pltpu.VMEM((2, tb, kw), BF16),
                        pltpu.VMEM((2, tb, kw), BF16), pltpu.VMEM((2, tb, kw), F32),
                        pltpu.VMEM((2, tb, vw), F32)],
        compiler_params=_cparams("parallel", "arbitrary"),
    )(qk, qk, v, gates, qk, qk, v, gates, s0t, *extra)


def _moe_kernel(idx_ref, gsm_ref, x_ref, mod_ref, gffn_ref, wg_ref, wu_ref, wd_ref, o_ref,
                xs_ref, gbuf_ref, ya_ref, yb_ref, *, cap, ne, nf):
    e = pl.program_id(1)
    f = pl.program_id(2)
    m = mod_ref[0]
    g2 = m[5:6]
    nsub = wg_ref.shape[1]
    rows_per = cap // (nf * nsub)
    e_next = jnp.minimum(e + 1, ne - 1)
    e_prev = jnp.maximum(e - 1, 0)

    def gather_row(en, r):
        gbuf_ref[pl.ds(r, 1), :] = x_ref[0, pl.ds(idx_ref[0, en, 0, r], 1), :]

    def scatter_row(ep, y_ref, r):
        i = idx_ref[0, ep, 0, r]
        o_ref[0, pl.ds(i, 1), :] += y_ref[pl.ds(r, 1), :] * gsm_ref[0, ep, 0, r] * g2

    @pl.when((e == 0) & (f == 0))
    def _():
        o_ref[...] = x_ref[...]
        yb_ref[...] = jnp.zeros(yb_ref.shape, F32)

        def body(r, carry):
            gather_row(0, r)
            return carry
        lax.fori_loop(0, cap, body, 0, unroll=8)

    @pl.when(f == 0)
    def _():
        xs_ref[...] = _rms_mod(gbuf_ref[...], gffn_ref[...], m[4:5], m[3:4]).astype(BF16)

    def expert_part(y_cur, y_prev):
        @pl.when(f == 0)
        def _():
            y_cur[...] = jnp.zeros(y_cur.shape, F32)

        def sub_chunk(k, carry):
            r0 = pl.multiple_of((f * nsub + k) * rows_per, rows_per)
            for r in range(rows_per):
                gather_row(e_next, r0 + r)
                scatter_row(e_prev, y_prev, r0 + r)
            xs = xs_ref[...]
            hid = _silu(_dot(xs, wg_ref[0, k])) * _dot(xs, wu_ref[0, k])
            y_cur[...] += _dot(hid.astype(BF16), wd_ref[0, k])
            return carry

        lax.fori_loop(0, nsub, sub_chunk, 0, unroll=True)

        @pl.when((e == ne - 1) & (f == nf - 1))
        def _():
            def body(r, carry):
                scatter_row(e, y_cur, r)
                return carry
            lax.fori_loop(0, cap, body, 0, unroll=8)

    @pl.when(e % 2 == 0)
    def _():
        expert_part(ya_ref, yb_ref)

    @pl.when(e % 2 == 1)
    def _():
        expert_part(yb_ref, ya_ref)


def _cast_kernel(w_ref, o_ref):
    o_ref[...] = w_ref[...].astype(BF16).reshape(o_ref.shape)


def _cast_cols_kernel(w_ref, o_ref):
    sub = o_ref.shape[-1]
    for k in range(o_ref.shape[2]):
        o_ref[0, 0, k] = w_ref[0, 0, :, k * sub:(k + 1) * sub].astype(BF16)


def _split_cast_cols(w, sub):
    nl, ne, d, ff = w.shape
    per = 1024 // sub
    return pl.pallas_call(
        _cast_cols_kernel,
        name="expert_weight_cols",
        grid=(nl, ne, ff // (per * sub)),
        in_specs=[pl.BlockSpec((1, 1, d, per * sub), lambda l, e, k: (l, e, 0, k))],
        out_specs=pl.BlockSpec((1, 1, per, d, sub), lambda l, e, k: (l, e, k, 0, 0)),
        out_shape=jax.ShapeDtypeStruct((nl, ne, ff // sub, d, sub), BF16),
        compiler_params=_cparams("parallel", "parallel", "parallel"),
    )(w)


def _split_cast_rows(w, sub):
    nl, ne, ff, d = w.shape
    rows = max(sub, 1024)
    return pl.pallas_call(
        _cast_kernel,
        name="expert_weight_rows",
        grid=(nl, ne, ff // rows),
        in_specs=[pl.BlockSpec((1, 1, rows, d), lambda l, e, k: (l, e, k, 0))],
        out_specs=pl.BlockSpec((1, 1, rows // sub, sub, d), lambda l, e, k: (l, e, k, 0, 0)),
        out_shape=jax.ShapeDtypeStruct((nl, ne, ff // sub, sub, d), BF16),
        compiler_params=_cparams("parallel", "parallel", "parallel"),
    )(w)


def _moe(x3, idx, gates, mod, g_ffn, w_gate, w_up, w_down, *, layer, nf):
    ns, ts, d = x3.shape
    _, ne, nsub_all, _, sub = w_gate.shape
    cap = idx.shape[2]
    nsub = nsub_all // nf
    return pl.pallas_call(
        functools.partial(_moe_kernel, cap=cap, ne=ne, nf=nf),
        name="moe",
        grid=(ns, ne, nf),
        in_specs=[
            pl.BlockSpec((1, ne, 1, cap), lambda s, e, f: (s, 0, 0, 0), memory_space=pltpu.SMEM),
            pl.BlockSpec((1, ne, 1, cap), lambda s, e, f: (s, 0, 0, 0), memory_space=pltpu.SMEM),
            pl.BlockSpec((1, ts, d), lambda s, e, f: (s, 0, 0), pipeline_mode=pl.Buffered(1)),
            pl.BlockSpec((1, 8, d), lambda s, e, f: (s, 0, 0)),
            pl.BlockSpec((1, d), lambda s, e, f: (0, 0)),
            pl.BlockSpec((None, 1, nsub, d, sub), lambda s, e, f: (layer, e, f, 0, 0)),
            pl.BlockSpec((None, 1, nsub, d, sub), lambda s, e, f: (layer, e, f, 0, 0)),
            pl.BlockSpec((None, 1, nsub, sub, d), lambda s, e, f: (layer, e, f, 0, 0)),
        ],
        out_specs=pl.BlockSpec((1, ts, d), lambda s, e, f: (s, 0, 0), pipeline_mode=pl.Buffered(1)),
        out_shape=jax.ShapeDtypeStruct((ns, ts, d), F32),
        scratch_shapes=[pltpu.VMEM((cap, d), BF16), pltpu.VMEM((cap, d), F32),
                        pltpu.VMEM((cap, d), F32), pltpu.VMEM((cap, d), F32)],
        compiler_params=_cparams("parallel", "arbitrary", "arbitrary"),
    )(idx.reshape(ns, ne, 1, cap), gates.reshape(ns, ne, 1, cap), x3, mod, g_ffn, w_gate, w_up, w_down)


def _route(aff, *, n_dec, ts, batch, seq, ne):
    cap_dec = EC_CAPACITY * ts // ne
    cap_ctx = EC_CAPACITY * seq // ne
    a = aff[:, :ne]
    a_dec = a[:n_dec * ts].reshape(n_dec, ts, ne).swapaxes(1, 2)
    g_dec, i_dec = lax.top_k(a_dec, cap_dec)
    a_ctx = a[n_dec * ts:].reshape(batch, seq, ne).swapaxes(1, 2)
    g_ctx, i_ctx = lax.top_k(a_ctx, cap_ctx)
    i_ctx = i_ctx + (jnp.arange(batch) * seq)[:, None, None]
    g_ctx = g_ctx.swapaxes(0, 1).reshape(1, ne, batch * cap_ctx)
    i_ctx = i_ctx.swapaxes(0, 1).reshape(1, ne, batch * cap_ctx)
    return (jnp.concatenate([i_dec, i_ctx], axis=0).astype(jnp.int32),
            jnp.concatenate([g_dec, g_ctx], axis=0))


def _final_kernel(x_ref, g_ref, o_ref):
    x = x_ref[...]
    ms = jnp.mean(x * x, axis=-1, keepdims=True)
    o_ref[...] = (x * lax.rsqrt(ms + RMS_EPS)) * g_ref[...]


def _final_norm(x, gain, *, row0, nrows, tm):
    d = x.shape[1]
    rb0 = row0 // tm
    return pl.pallas_call(
        _final_kernel,
        name="final_norm",
        grid=(nrows // tm,),
        in_specs=[pl.BlockSpec((tm, d), lambda i: (rb0 + i, 0)),
                  pl.BlockSpec((1, d), lambda i: (0, 0))],
        out_specs=pl.BlockSpec((tm, d), lambda i: (i, 0)),
        out_shape=jax.ShapeDtypeStruct((nrows, d), F32),
        compiler_params=_cparams("parallel"),
    )(x, gain)


def kernel(x_prompt, x_sample, c, c_ctx, cache_attn_k, cache_attn_v, state_s5, state_gla, norm_mix, norm_ffn, w_mod, b_mod, w_router, ffn_w_gate, ffn_w_up, ffn_w_down, ab_w_in, ab_w_out, attn_q_norm, attn_k_norm, s5_lambda_re, s5_lambda_im, s5_log_dt, s5_b_re, s5_b_im, s5_c_re, s5_c_im, s5_d, s5_w_glu, s5_b_glu, gla_w_in, gla_w_gate_up, gla_b_gate, gla_o_norm, gla_w_out, norm_out):
    batch, seq, d = x_prompt.shape
    n_dec, ts, _ = x_sample.shape
    assert batch * seq == ts, "context batch must flatten to one latent-length sequence"
    depth = w_mod.shape[0]
    assert depth == 2 and ab_w_in.shape[0] == 1 and gla_w_in.shape[0] == 1
    ns = n_dec + 1
    ntok = ns * ts
    n_dec_tok = n_dec * ts
    ne = w_router.shape[2]
    tm = min(TOKEN_TILE, ts)
    tps = ts // tm
    qw = ATTN_HEADS * HEAD_DIM
    kw = ATTN_KV_HEADS * HEAD_DIM
    s5w = d // 2
    s5g = s5w // S5_GROUP_CH
    s5n = s5_lambda_re.shape[-1]
    past = cache_attn_k.shape[2]

    x_dec = x_sample.reshape(n_dec_tok, d)
    x_ctx = x_prompt.reshape(batch * seq, d)

    rows = -(-ns // 8) * 8
    cond = jnp.concatenate([c, c_ctx[None, :], jnp.zeros((rows - ns, d), F32)], axis=0)
    mod = _modulation(cond, w_mod, b_mod)[:, :ns].reshape(depth, ns, 6, d)
    mod = jnp.pad(mod, ((0, 0), (0, 0), (0, 2), (0, 0)))

    wr_pad = jnp.pad(w_router, ((0, 0), (0, 0), (0, LANES - ne)))
    wg_bf = _split_cast_cols(ffn_w_gate, MOE_F_SUB)
    wu_bf = _split_cast_cols(ffn_w_up, MOE_F_SUB)
    wd_bf = _split_cast_rows(ffn_w_down, MOE_F_SUB)

    def moe_layer(xin, aff, layer):
        idx, gates = _route(aff, n_dec=n_dec, ts=ts, batch=batch, seq=seq, ne=ne)
        out = _moe(xin.reshape(ns, ts, d), idx, gates, mod[layer], norm_ffn[layer:layer + 1],
                   wg_bf, wu_bf, wd_bf, layer=layer, nf=MOE_F_PARTS)
        return out.reshape(ntok, d)

    cos_tab, sin_tab = _rope_tables(ts)
    q_bf, k_bf, k_nrm, v_bf, v_f32, u = _inproj_ab(
        x_dec, x_ctx, mod[0], norm_mix[0:1], ab_w_in[0].astype(BF16), cos_tab, sin_tab,
        attn_q_norm[0], attn_k_norm[0], ts=ts, n_rope_seq=n_dec, tm=tm)
    keys = jnp.concatenate([cache_attn_k[:, 0].reshape(n_dec, past, kw).astype(BF16),
                            k_bf[:n_dec_tok].reshape(n_dec, ts, kw)], axis=1)
    vals = jnp.concatenate([cache_attn_v[:, 0].reshape(n_dec, past, kw).astype(BF16),
                            v_bf[:n_dec_tok].reshape(n_dec, ts, kw)], axis=1)
    o_attn = _attention(q_bf, keys.transpose(0, 2, 1), vals, None, row0=0, t=ts, tq=min(ATTN_Q_TILE, ts))
    o_attn = _attention(q_bf, k_bf[n_dec_tok:].reshape(batch, seq, kw).transpose(0, 2, 1),
                        v_bf[n_dec_tok:].reshape(batch, seq, kw), o_attn,
                        row0=n_dec_tok, t=seq, tq=min(ATTN_Q_TILE, seq))

    mats = _s5_matrices(s5_lambda_re[0], s5_lambda_im[0], s5_log_dt[0], s5_b_re[0], s5_b_im[0],
                        s5_c_re[0], s5_c_im[0], s5_d[0])
    h0 = state_s5[:, 0].transpose(2, 1, 4, 0, 3).reshape(s5g, 4, n_dec, s5n)
    h0 = jnp.pad(h0, ((0, 0), (0, 0), (0, 0), (0, LANES - s5n)))
    seqs = ((n_dec, ts // S5_CHUNK, 0, True), (batch, seq // S5_CHUNK, n_dec_tok // S5_CHUNK, False))
    y, hfin = _s5(u, *mats, h0, p=S5_GROUP_CH, seqs=seqs)

    w_out = ab_w_out[0].astype(BF16)
    x, aff = _ab_post(y, o_attn, x_dec, x_ctx, mod[0], s5_w_glu[0].astype(BF16), s5_b_glu[0][None, :],
                      w_out[:qw], w_out[qw:], norm_ffn[0:1], wr_pad[0], tps=tps, tm=tm, ne=ne)
    x = moe_layer(x, aff, 0)

    nh = GLA_HEADS
    dk = d // 2 // nh
    dv = d // nh
    main_w = 2 * nh * dk + 2 * nh * dv
    w_gla = gla_w_in[0]
    wl = jnp.pad(w_gla[:, main_w:], ((0, 0), (0, LANES - 2 * GLA_RANK))).astype(BF16)
    wgu = jnp.zeros((LANES, 2 * nh * dk), F32)
    wgu = wgu.at[:GLA_RANK, :nh * dk].set(gla_w_gate_up[0, 0])
    wgu = wgu.at[GLA_RANK:2 * GLA_RANK, nh * dk:].set(gla_w_gate_up[0, 1])
    bgu = jnp.concatenate([gla_b_gate[0, 0], gla_b_gate[0, 1]])[None, :]
    qk1, v1, r1, gates1 = _inproj_gla(x, mod[1], norm_mix[1:2], w_gla[:, :main_w].astype(BF16), wl, wgu, bgu,
                                      widths=(2 * nh * dk, nh * dv, nh * dv), tps=tps, tm=tm)
    s0t = state_gla[:, 0].swapaxes(-1, -2)
    o_f, o_b, _ = _gla(qk1, v1, gates1, s0t, None, row0=0, nb=n_dec, t=ts, tb=min(GLA_BLOCK, ts), dk=dk, dv=dv)
    o_f, o_b, sfin = _gla(qk1, v1, gates1, jnp.zeros((batch, 2, nh, dv, dk), F32), (o_f, o_b),
                          row0=n_dec_tok, nb=batch, t=seq, tb=min(GLA_BLOCK, seq), dk=dk, dv=dv)
    x, aff = _gla_post(o_f, o_b, r1, x, mod[1], gla_o_norm[0][None, :], gla_w_out[0].astype(BF16),
                       norm_ffn[1:2], wr_pad[1], tps=tps, tm=tm, ne=ne)
    x = moe_layer(x, aff, 1)

    gain = norm_out[None, :]
    tn = min(4 * tm, ts)
    y_sample = _final_norm(x, gain, row0=0, nrows=n_dec_tok, tm=tn).reshape(n_dec, ts, d)
    y_prompt = _final_norm(x, gain, row0=n_dec_tok, nrows=batch * seq, tm=tn).reshape(batch, seq, d)
    new_k = k_nrm[n_dec_tok:].reshape(batch, 1, seq, ATTN_KV_HEADS, HEAD_DIM)
    new_v = v_f32[n_dec_tok:].reshape(batch, 1, seq, ATTN_KV_HEADS, HEAD_DIM)
    new_s5 = hfin[..., :s5n].reshape(s5g, 2, 2, batch, s5n).transpose(3, 1, 0, 4, 2)[:, None]
    new_gla = sfin.swapaxes(-1, -2)[:, None]
    return (y_prompt, y_sample, new_k, new_v, new_s5, new_gla)
```
